```python
import math
import jax
import jax.numpy as jnp
from jax import lax
import numpy as np

D_MODEL = 1024
BATCH = 2
SEQ = 8192
DEPTH = 2

GRID_W = 64
CTX_LEN = 256
N_DIRS = 2
NORM_EPS = 1e-6
D_MIX = D_MODEL

DA_HEADS = 4
DA_QK_DIM = 64
DA_V_DIM = 2 * DA_QK_DIM
DA_WIDTH = DA_HEADS * DA_V_DIM
DA_Q_BLOCK = 128
ROPE_BASE = 10000.0
ROPE_FREQS = DA_QK_DIM // 4

GDN_HEADS = 4
GDN_HEAD_DIM = 64
GDN_WIDTH = GDN_HEADS * GDN_HEAD_DIM
GDN_CONV = 5
GDN_CHUNK = 64

RWKV_HEADS = 4
RWKV_HEAD_DIM = 64
RWKV_WIDTH = RWKV_HEADS * RWKV_HEAD_DIM
RWKV_W_LORA = 32
RWKV_A_LORA = 32
RWKV_G_LORA = 64
RWKV_DECAY_SCALE = 0.6065306597126334
RWKV_GN_EPS = 64e-5

N_EXPERTS = 16
EC_CAPACITY_FACTOR = 2
D_EXPERT = 1024

IN_SIZES = (DA_HEADS * 2 * DA_QK_DIM, DA_HEADS * 2 * DA_QK_DIM, DA_WIDTH,
            3 * GDN_WIDTH, GDN_WIDTH, N_DIRS * GDN_HEADS, N_DIRS * GDN_HEADS,
            3 * RWKV_WIDTH, RWKV_W_LORA, RWKV_A_LORA, RWKV_G_LORA)
D_IN = sum(IN_SIZES)

kernel_name = 'hybrid_parallel_heads_diffusion_trunk'


def _rmsnorm(x, g):
    xf = x.astype(jnp.float32)
    y = xf * lax.rsqrt(jnp.mean(xf * xf, axis=-1, keepdims=True) + NORM_EPS)
    return (y * g.astype(jnp.float32)).astype(x.dtype)


def _l2norm(x):
    return x * lax.rsqrt(jnp.sum(x * x, axis=-1, keepdims=True) + 1e-12)


def _flip(t):
    return jnp.flip(t, axis=1)


def _ident(t):
    return t


def _modulation(cond, w, b, dtype):
    m = jnp.dot(cond, w.astype(jnp.float32)) + b.astype(jnp.float32)
    return [t.astype(dtype) for t in jnp.split(m[:, None, :], 6, axis=-1)]


def _modulate(h, shift, scale):
    return h * (1 + scale) + shift


def _split_cols(p):
    out, off = [], 0
    for n in IN_SIZES:
        out.append(p[..., off:off + n])
        off += n
    return out


def _axial_rope_tables(rows):
    row = jnp.repeat(jnp.arange(rows, dtype=jnp.float32), GRID_W)
    col = jnp.tile(jnp.arange(GRID_W, dtype=jnp.float32), rows)
    freqs = ROPE_BASE ** (-jnp.arange(ROPE_FREQS, dtype=jnp.float32) / ROPE_FREQS)
    ang = jnp.stack([row[:, None] * freqs, col[:, None] * freqs], axis=1)
    return jnp.cos(ang), jnp.sin(ang)


def _rope_2d(x, cos, sin):
    shp = x.shape
    xs = x.astype(jnp.float32).reshape(shp[:-1] + (2, 2, ROPE_FREQS))
    x1, x2 = xs[..., 0, :], xs[..., 1, :]
    cb, sb = cos[None, :, None, None], sin[None, :, None, None]
    y = jnp.stack([x1 * cb - x2 * sb, x2 * cb + x1 * sb], axis=-2)
    return y.reshape(shp).astype(x.dtype)


def _da_heads(p):
    B, T, _ = p[0].shape
    q = p[0].reshape(B, T, DA_HEADS, 2, DA_QK_DIM)
    k = p[1].reshape(B, T, DA_HEADS, 2, DA_QK_DIM)
    v = p[2].reshape(B, T, DA_HEADS, DA_V_DIM)
    return q, k, v


def _diff_lambda(lv, lam_init):
    lv = lv.astype(jnp.float32)
    return jnp.exp(jnp.sum(lv[0] * lv[1])) - jnp.exp(jnp.sum(lv[2] * lv[3])) + lam_init


def _diff_attention(q_c, k_c, v_c, q_l, k_l, v_l, cos, sin, lam, lam_init, norm_g, need_ctx):
    dtype = v_l.dtype
    scale = DA_QK_DIM ** -0.5
    q_l = _rope_2d(q_l, cos, sin)
    k_l = _rope_2d(k_l, cos, sin)
    k_all = jnp.concatenate([k_c, k_l], axis=1)
    v_all = jnp.concatenate([v_c, v_l], axis=1).astype(jnp.float32)

    def attend(q, k, v):
        s = jnp.einsum('bqhmd,bkhmd->bhmqk', q, k, preferred_element_type=jnp.float32) * scale
        p = jax.nn.softmax(s, axis=-1)
        w = p[:, :, 0] - lam * p[:, :, 1]
        return jnp.einsum('bhqk,bkhd->bqhd', w, v)

    def finish(o):
        B, T = o.shape[:2]
        return (_rmsnorm(o, norm_g) * (1.0 - lam_init)).reshape(B, T, DA_WIDTH).astype(dtype)

    B, T = q_l.shape[:2]
    n_blk = T // DA_Q_BLOCK
    q_blocks = jnp.swapaxes(q_l.reshape((B, n_blk, DA_Q_BLOCK) + q_l.shape[2:]), 0, 1)
    o_l = lax.map(lambda qb: attend(qb, k_all, v_all), q_blocks)
    o_l = jnp.swapaxes(o_l, 0, 1).reshape(B, T, DA_HEADS, DA_V_DIM)
    o_c = finish(attend(q_c, k_c, v_c.astype(jnp.float32))) if need_ctx else None
    return o_c, finish(o_l)


def _short_conv(x, w):
    pad = GDN_CONV // 2
    y = lax.conv_general_dilated(x, w[:, None, :].astype(x.dtype), window_strides=(1,), padding=[(pad, pad)],
                                 dimension_numbers=('NWC', 'WIO', 'NWC'), feature_group_count=x.shape[-1])
    return jax.nn.silu(y)


def _gdn_prepare(qkv, a_in, b_in, conv_w, a_log, dt_bias):
    B, T, _ = qkv.shape
    qkv = _short_conv(qkv, conv_w).astype(jnp.float32).reshape(B, T, 3, GDN_HEADS, GDN_HEAD_DIM)
    q = _l2norm(qkv[:, :, 0]) * GDN_HEAD_DIM ** -0.5
    k = _l2norm(qkv[:, :, 1])
    v = qkv[:, :, 2]
    a = a_in.astype(jnp.float32).reshape(B, T, N_DIRS, GDN_HEADS)
    g = -jnp.exp(a_log.astype(jnp.float32)) * jax.nn.softplus(a + dt_bias.astype(jnp.float32))
    beta = jax.nn.sigmoid(b_in.astype(jnp.float32).reshape(B, T, N_DIRS, GDN_HEADS))
    return q, k, v, g, beta


def _gdn_chunked(q, k, v, g, beta, s0):
    B, T, H, _ = q.shape
    C = GDN_CHUNK
    n = T // C

    def chunks(t):
        t = t.reshape((B, n, C, H) + t.shape[3:])
        return jnp.moveaxis(t, (1, 3), (0, 2))

    qc, kc, vc, gch, bch = (chunks(t) for t in (q, k, v, g, beta))
    gcum = jnp.cumsum(gch, axis=-1)
    incl = jnp.tril(jnp.ones((C, C), dtype=bool))
    strict = jnp.tril(jnp.ones((C, C), dtype=bool), -1)
    decay = jnp.exp(jnp.where(incl, gcum[..., :, None] - gcum[..., None, :], -jnp.inf))
    kb = kc * bch[..., None]
    a_mat = jnp.where(strict, jnp.einsum('nbhid,nbhjd->nbhij', kb, kc) * decay, 0.0)
    eye = jnp.eye(C, dtype=jnp.float32)
    t_inv = lax.linalg.triangular_solve(eye + a_mat, jnp.broadcast_to(eye, a_mat.shape),
                                        left_side=True, lower=True, unit_diagonal=True)
    u = jnp.einsum('nbhij,nbhjd->nbhid', t_inv, vc * bch[..., None])
    w = jnp.einsum('nbhij,nbhjd->nbhid', t_inv, kb * jnp.exp(gcum)[..., None])
    qk = jnp.einsum('nbhid,nbhjd->nbhij', qc, kc) * decay
    q_dec = qc * jnp.exp(gcum)[..., None]
    k_dec = kc * jnp.exp(gcum[..., -1:] - gcum)[..., None]
    g_tot = jnp.exp(gcum[..., -1])

    def step(s, xs):
        u_i, w_i, qk_i, qd_i, kd_i, gt_i = xs
        v_new = u_i - jnp.einsum('bhid,bhdv->bhiv', w_i, s)
        o_i = jnp.einsum('bhid,bhdv->bhiv', qd_i, s) + jnp.einsum('bhij,bhjv->bhiv', qk_i, v_new)
        s = s * gt_i[..., None, None] + jnp.einsum('bhid,bhiv->bhdv', kd_i, v_new)
        return s, o_i

    s_fin, o = lax.scan(step, s0, (u, w, qk, q_dec, k_dec, g_tot))
    o = jnp.moveaxis(o, (0, 2), (1, 3)).reshape(B, T, H, v.shape[-1])
    return s_fin, o


def _gdn_out(o, gate, norm_g, dtype):
    B, T = o.shape[:2]
    y = _rmsnorm(o, norm_g) * jax.nn.silu(gate.astype(jnp.float32).reshape(B, T, GDN_HEADS, GDN_HEAD_DIM))
    return y.reshape(B, T, GDN_WIDTH).astype(dtype)


def _gdn_mixer(p_c, p_l, conv_w, a_log, dt_bias, norm_g):
    dtype = p_l[3].dtype
    qc, kc, vc, gc, bc = _gdn_prepare(p_c[3], p_c[5], p_c[6], conv_w, a_log, dt_bias)
    ql, kl, vl, gl, bl = _gdn_prepare(p_l[3], p_l[5], p_l[6], conv_w, a_log, dt_bias)
    s0 = jnp.zeros((ql.shape[0], GDN_HEADS, GDN_HEAD_DIM, GDN_HEAD_DIM), jnp.float32)
    o_c, o_l = [], []
    for d in range(N_DIRS):
        f = _flip if d == 1 else _ident
        s_c, oc = _gdn_chunked(f(qc), f(kc), f(vc), f(gc[:, :, d]), f(bc[:, :, d]), s0)
        _, ol = _gdn_chunked(f(ql), f(kl), f(vl), f(gl[:, :, d]), f(bl[:, :, d]), s_c)
        o_c.append(f(oc))
        o_l.append(f(ol))
    return (_gdn_out(o_c[0] + o_c[1], p_c[4], norm_g, dtype),
            _gdn_out(o_l[0] + o_l[1], p_l[4], norm_g, dtype))


def _rwkv_prepare(rkv, xw, xa, xg, w0, w_up, a0, a_up, g_up, k_k, k_a):
    rkv, xw, xa, xg = (t.astype(jnp.float32) for t in (rkv, xw, xa, xg))
    B, T, _ = rkv.shape

    def heads(t):
        return t.reshape(B, T, RWKV_HEADS, RWKV_HEAD_DIM)

    r, k, v = rkv[..., :RWKV_WIDTH], rkv[..., RWKV_WIDTH:2 * RWKV_WIDTH], rkv[..., 2 * RWKV_WIDTH:]
    kk = _l2norm(heads(k * k_k))
    g = jnp.dot(jax.nn.sigmoid(xg), g_up.astype(jnp.float32))
    decay, a, kd = [], [], []
    for d in range(N_DIRS):
        logw = -RWKV_DECAY_SCALE * jax.nn.sigmoid(w0[d] + jnp.dot(jnp.tanh(xw), w_up[d].astype(jnp.float32)))
        a_d = jax.nn.sigmoid(a0[d] + jnp.dot(xa, a_up[d].astype(jnp.float32)))
        decay.append(heads(jnp.exp(logw)))
        a.append(heads(a_d))
        kd.append(heads(k * (1.0 + (a_d - 1.0) * k_a)))
    return heads(r), heads(v), kk, g, decay, a, kd


def _rwkv_scan(r, w, k, v, kk, a, s0):
    def step(s, xs):
        r_t, w_t, k_t, v_t, kk_t, a_t = xs
        read = jnp.einsum('bhk,bhkv->bhv', -kk_t, s)
        s = w_t[..., :, None] * s + (kk_t * a_t)[..., :, None] * read[..., None, :] + k_t[..., :, None] * v_t[..., None, :]
        return s, jnp.einsum('bhk,bhkv->bhv', r_t, s)

    xs = tuple(jnp.moveaxis(t, 1, 0) for t in (r, w, k, v, kk, a))
    s_fin, y = lax.scan(step, s0, xs)
    return s_fin, jnp.moveaxis(y, 0, 1)


def _rwkv_out(y, prep, r_k, gn_w, gn_b, dtype):
    r, v, _, g, _, _, kd = prep
    B, T = y.shape[:2]
    mu = jnp.mean(y, axis=-1, keepdims=True)
    var = jnp.mean(jnp.square(y - mu), axis=-1, keepdims=True)
    yn = ((y - mu) * lax.rsqrt(var + RWKV_GN_EPS)).reshape(B, T, RWKV_WIDTH) * gn_w + gn_b
    bonus = (jnp.sum(r * (kd[0] + kd[1]) * r_k, axis=-1, keepdims=True) * v).reshape(B, T, RWKV_WIDTH)
    return ((yn + bonus) * g).astype(dtype)


def _rwkv_mixer(p_c, p_l, w0, w_up, a0, a_up, g_up, k_k, k_a, r_k, gn_w, gn_b):
    dtype = p_l[7].dtype
    prm = (w0, w_up, a0, a_up, g_up, k_k, k_a)
    c_in = _rwkv_prepare(p_c[7], p_c[8], p_c[9], p_c[10], *prm)
    l_in = _rwkv_prepare(p_l[7], p_l[8], p_l[9], p_l[10], *prm)
    rc, vc, kkc, _, wc, ac, kc = c_in
    rl, vl, kkl, _, wl, al, kl = l_in
    s0 = jnp.zeros((rl.shape[0], RWKV_HEADS, RWKV_HEAD_DIM, RWKV_HEAD_DIM), jnp.float32)
    y_c, y_l = [], []
    for d in range(N_DIRS):
        f = _flip if d == 1 else _ident
        s_c, yc = _rwkv_scan(f(rc), f(wc[d]), f(kc[d]), f(vc), f(kkc), f(ac[d]), s0)
        _, yl = _rwkv_scan(f(rl), f(wl[d]), f(kl[d]), f(vl), f(kkl), f(al[d]), s_c)
        y_c.append(f(yc))
        y_l.append(f(yl))
    return (_rwkv_out(y_c[0] + y_c[1], c_in, r_k, gn_w, gn_b, dtype),
            _rwkv_out(y_l[0] + y_l[1], l_in, r_k, gn_w, gn_b, dtype))


def _expert_choice_moe(h, router, w_gate, w_up, w_down):
    B, T, _ = h.shape
    cap = EC_CAPACITY_FACTOR * T // N_EXPERTS
    logits = jnp.einsum('btd,de->bte', h, router, preferred_element_type=jnp.float32)
    aff = jax.nn.softmax(logits, axis=-1)
    top_aff, top_idx = lax.top_k(jnp.swapaxes(aff, 1, 2), cap)
    b_idx = jnp.arange(B)[:, None, None]
    xe = h[b_idx, top_idx]
    hid = jax.nn.silu(jnp.einsum('becd,edf->becf', xe, w_gate)) * jnp.einsum('becd,edf->becf', xe, w_up)
    ye = jnp.einsum('becf,efd->becd', hid, w_down) * top_aff[..., None].astype(h.dtype)
    return jnp.zeros_like(h).at[b_idx, top_idx].add(ye.astype(h.dtype))


def setup_inputs(seed: int = 0) -> dict:
    key = jax.random.key(seed)
    ks = jax.random.split(key, 32)
    f32 = jnp.float32
    D = D_MODEL

    def nrm(i, shape, s):
        return jax.random.normal(ks[i], shape, f32) * s

    dt = jnp.exp(jax.random.uniform(ks[14], (DEPTH, N_DIRS, GDN_HEADS), f32, math.log(1e-3), math.log(1e-1)))
    return {
        'x': nrm(0, (BATCH, SEQ, D), 1.0),
        'c': nrm(1, (BATCH, D), 1.0),
        'ctx': nrm(2, (BATCH, CTX_LEN, D), 1.0),
        'c_ctx': nrm(3, (D,), 1.0),
        'w_mod': nrm(4, (DEPTH, D, 6 * D), 0.5 * D ** -0.5),
        'b_mod': nrm(5, (DEPTH, 6 * D), 0.02),
        'norm_pre': 1.0 + nrm(6, (DEPTH, 2, D), 0.02),
        'norm_post': 1.0 + nrm(7, (DEPTH, 2, D), 0.02),
        'w_in': nrm(8, (DEPTH, D, D_IN), D ** -0.5),
        'w_out': nrm(9, (DEPTH, D_MIX, D), D_MIX ** -0.5),
        'da_lambda': nrm(10, (DEPTH, 4, DA_QK_DIM), 0.1),
        'da_norm': 1.0 + nrm(11, (DEPTH, DA_V_DIM), 0.02),
        'gdn_conv': nrm(12, (DEPTH, GDN_CONV, 3 * GDN_WIDTH), GDN_CONV ** -0.5),
        'gdn_a_log': jnp.log(jax.random.uniform(ks[13], (DEPTH, N_DIRS, GDN_HEADS), f32, 1.0, 16.0)),
        'gdn_dt_bias': dt + jnp.log(-jnp.expm1(-dt)),
        'gdn_norm': 1.0 + nrm(15, (DEPTH, GDN_HEAD_DIM), 0.02),
        'rwkv_w0': nrm(16, (DEPTH, N_DIRS, RWKV_WIDTH), 1.0),
        'rwkv_w_up': nrm(17, (DEPTH, N_DIRS, RWKV_W_LORA, RWKV_WIDTH), 0.1),
        'rwkv_a0': nrm(18, (DEPTH, N_DIRS, RWKV_WIDTH), 0.5),
        'rwkv_a_up': nrm(19, (DEPTH, N_DIRS, RWKV_A_LORA, RWKV_WIDTH), 0.1),
        'rwkv_g_up': nrm(20, (DEPTH, RWKV_G_LORA, RWKV_WIDTH), RWKV_G_LORA ** -0.5),
        'rwkv_k_k': 0.85 + nrm(21, (DEPTH, RWKV_WIDTH), 0.02),
        'rwkv_k_a': 1.0 + nrm(22, (DEPTH, RWKV_WIDTH), 0.02),
        'rwkv_r_k': nrm(23, (DEPTH, RWKV_HEADS, RWKV_HEAD_DIM), 0.1),
        'rwkv_gn_w': 1.0 + nrm(24, (DEPTH, RWKV_WIDTH), 0.02),
        'rwkv_gn_b': nrm(25, (DEPTH, RWKV_WIDTH), 0.02),
        'moe_router': nrm(26, (DEPTH, D, N_EXPERTS), D ** -0.5),
        'moe_w_gate': nrm(27, (DEPTH, N_EXPERTS, D, D_EXPERT), D ** -0.5),
        'moe_w_up': nrm(28, (DEPTH, N_EXPERTS, D, D_EXPERT), D ** -0.5),
        'moe_w_down': nrm(29, (DEPTH, N_EXPERTS, D_EXPERT, D), D_EXPERT ** -0.5),
    }


def reference(x, c, ctx, c_ctx, w_mod, b_mod, norm_pre, norm_post, w_in, w_out,
              da_lambda, da_norm, gdn_conv, gdn_a_log, gdn_dt_bias, gdn_norm,
              rwkv_w0, rwkv_w_up, rwkv_a0, rwkv_a_up, rwkv_g_up, rwkv_k_k, rwkv_k_a,
              rwkv_r_k, rwkv_gn_w, rwkv_gn_b, moe_router, moe_w_gate, moe_w_up, moe_w_down):
    dtype = x.dtype
    T = x.shape[1]
    ROWS = T // GRID_W
    cos, sin = _axial_rope_tables(ROWS)
    cond_lat = jax.nn.silu(c.astype(jnp.float32))
    cond_ctx = jax.nn.silu(c_ctx.astype(jnp.float32))[None, :]
    x_lat, x_ctx = x, ctx.astype(dtype)
    for i in range(DEPTH):
        last = i == DEPTH - 1
        sh1_l, sc1_l, gt1_l, sh2_l, sc2_l, gt2_l = _modulation(cond_lat, w_mod[i], b_mod[i], dtype)
        sh1_c, sc1_c, gt1_c, sh2_c, sc2_c, gt2_c = _modulation(cond_ctx, w_mod[i], b_mod[i], dtype)

        h_c = _modulate(_rmsnorm(x_ctx, norm_pre[i, 0]), sh1_c, sc1_c)
        h_l = _modulate(_rmsnorm(x_lat, norm_pre[i, 0]), sh1_l, sc1_l)
        p_c = _split_cols(h_c @ w_in[i])
        p_l = _split_cols(h_l @ w_in[i])
        lam_init = 0.8 - 0.6 * math.exp(-0.3 * i)
        lam = _diff_lambda(da_lambda[i], lam_init)
        a_c, a_l = _diff_attention(*_da_heads(p_c), *_da_heads(p_l), cos, sin, lam, lam_init, da_norm[i], not last)
        g_c, g_l = _gdn_mixer(p_c, p_l, gdn_conv[i], gdn_a_log[i], gdn_dt_bias[i], gdn_norm[i])
        r_c, r_l = _rwkv_mixer(p_c, p_l, rwkv_w0[i], rwkv_w_up[i], rwkv_a0[i], rwkv_a_up[i], rwkv_g_up[i],
                               rwkv_k_k[i], rwkv_k_a[i], rwkv_r_k[i], rwkv_gn_w[i], rwkv_gn_b[i])
        mix_l = jnp.concatenate([a_l, g_l, r_l], axis=-1) @ w_out[i]
        x_lat = x_lat + gt1_l * _rmsnorm(mix_l, norm_post[i, 0])
        if not last:
            mix_c = jnp.concatenate([a_c, g_c, r_c], axis=-1) @ w_out[i]
            x_ctx = x_ctx + gt1_c * _rmsnorm(mix_c, norm_post[i, 0])

        h_l = _modulate(_rmsnorm(x_lat, norm_pre[i, 1]), sh2_l, sc2_l)
        y_l = _expert_choice_moe(h_l, moe_router[i], moe_w_gate[i], moe_w_up[i], moe_w_down[i])
        x_lat = x_lat + gt2_l * _rmsnorm(y_l, norm_post[i, 1])
        if not last:
            h_c = _modulate(_rmsnorm(x_ctx, norm_pre[i, 1]), sh2_c, sc2_c)
            y_c = _expert_choice_moe(h_c, moe_router[i], moe_w_gate[i], moe_w_up[i], moe_w_down[i])
            x_ctx = x_ctx + gt2_c * _rmsnorm(y_c, norm_post[i, 1])
    return x_lat
```

```python
import functools
import math

import jax
import jax.numpy as jnp
from jax import lax
from jax.experimental import pallas as pl
from jax.experimental.pallas import tpu as pltpu

F32 = jnp.float32
BF16 = jnp.bfloat16
I32 = jnp.int32
HI = lax.Precision.HIGHEST

GRID_W = 64
NORM_EPS = 1e-6
DA_HEADS = 4
DA_QK_DIM = 64
DA_V_DIM = 128
ROPE_BASE = 10000.0
ROPE_FREQS = DA_QK_DIM // 4
GDN_HEADS = 4
GDN_HEAD_DIM = 64
GDN_WIDTH = GDN_HEADS * GDN_HEAD_DIM
GDN_CONV = 5
CHUNK = 64
RWKV_HEADS = 4
RWKV_HEAD_DIM = 64
RWKV_WIDTH = RWKV_HEADS * RWKV_HEAD_DIM
RWKV_W_LORA = 32
RWKV_A_LORA = 32
RWKV_G_LORA = 64
RWKV_DECAY_SCALE = 0.6065306597126334
RWKV_GN_EPS = 64e-5
N_EXPERTS = 16
EC_CAPACITY_FACTOR = 2
N_DIRS = 2
IN_SIZES = (512, 512, 512, 768, 256, 8, 8, 768, 32, 32, 64)
D_IN = sum(IN_SIZES)
LANES = 128
VMEM_LIMIT = 56 * 1024 * 1024


def _pick(n, cands):
    for c in cands:
        if n % c == 0:
            return c
    raise ValueError(f"no tile for {n} in {cands}")


def _cparams(sem):
    return pltpu.CompilerParams(dimension_semantics=sem, vmem_limit_bytes=VMEM_LIMIT)


def _rms(x, g):
    return x * lax.rsqrt(jnp.mean(x * x, axis=-1, keepdims=True) + NORM_EPS) * g


def _row_mod(mods_ref, idx, row0, n_rows, n_ctx):
    rows = row0 + lax.broadcasted_iota(I32, (n_rows, 1), 0)
    return jnp.where(rows < n_ctx, mods_ref[0, 0, idx:idx + 1, :], mods_ref[0, 1, idx:idx + 1, :])


def _inproj_kernel(x_ref, g_ref, mods_ref, w_ref, o_ref, *, tm, n_ctx):
    row0 = pl.program_id(2) * tm
    x = x_ref[0]
    h = _rms(x, g_ref[...])
    h = h * (1.0 + _row_mod(mods_ref, 1, row0, tm, n_ctx)) + _row_mod(mods_ref, 0, row0, tm, n_ctx)
    o_ref[0] = jnp.dot(h.astype(BF16), w_ref[...], preferred_element_type=F32)


def _inproj(xa, g, mods, w, n_ctx):
    B, R, D = xa.shape
    N = w.shape[1]
    tm = _pick(R, (768, 512, 256, 128, 64))
    nb = _pick(N, (1792, 896, 512, 256, 128))
    return pl.pallas_call(
        functools.partial(_inproj_kernel, tm=tm, n_ctx=n_ctx),
        grid=(N // nb, B, R // tm),
        in_specs=[
            pl.BlockSpec((1, tm, D), lambda n, b, i: (b, i, 0)),
            pl.BlockSpec((1, D), lambda n, b, i: (0, 0)),
            pl.BlockSpec((1, 2, 6, D), lambda n, b, i: (b, 0, 0, 0)),
            pl.BlockSpec((D, nb), lambda n, b, i: (0, n)),
        ],
        out_specs=pl.BlockSpec((1, tm, nb), lambda n, b, i: (b, i, n)),
        out_shape=jax.ShapeDtypeStruct((B, R, N), F32),
        compiler_params=_cparams(("parallel", "parallel", "parallel")),
    )(xa, g.reshape(1, D), mods, w)


def _attn_kernel(lam_ref, q_ref, k_ref, v_ref, g_ref, o_ref, *, tq, tk, n_kv, out_scale):
    q = q_ref[0]
    lane = lax.broadcasted_iota(I32, q.shape, 1)
    zero = jnp.zeros_like(q)
    qq = jnp.concatenate([jnp.where(lane < DA_QK_DIM, q, zero),
                          jnp.where(lane >= DA_QK_DIM, q, zero)], axis=0)

    def body(c, carry):
        m, l, acc = carry
        start = pl.multiple_of(c * tk, tk)
        k = k_ref[0, pl.ds(start, tk), :]
        v = v_ref[0, pl.ds(start, tk), :]
        s = lax.dot_general(qq, k, (((1,), (1,)), ((), ())), preferred_element_type=F32)
        m_new = jnp.maximum(m, jnp.max(s, axis=-1, keepdims=True))
        alpha = jnp.exp(m - m_new)
        p = jnp.exp(s - m_new)
        l = alpha * l + jnp.sum(p, axis=-1, keepdims=True)
        acc = alpha * acc + jnp.dot(p.astype(BF16), v, preferred_element_type=F32)
        return m_new, l, acc

    m0 = jnp.full((2 * tq, 1), -jnp.inf, F32)
    l0 = jnp.zeros((2 * tq, 1), F32)
    a0 = jnp.zeros((2 * tq, DA_V_DIM), F32)
    m, l, acc = lax.fori_loop(0, n_kv, body, (m0, l0, a0))
    o = acc / l
    o = o[:tq] - lam_ref[0] * o[tq:]
    o_ref[0] = _rms(o, g_ref[...]) * out_scale


def _attention(q, k, v, lam, g, *, q_row0, n_q, n_kv_rows, out_scale):
    B, R, _ = q.shape
    tq = _pick(math.gcd(n_q, q_row0) if q_row0 else n_q, (256, 128, 64))
    tk = _pick(n_kv_rows, (768, 512, 256, 128, 64))
    qoff = q_row0 // tq
    return pl.pallas_call(
        functools.partial(_attn_kernel, tq=tq, tk=tk, n_kv=n_kv_rows // tk, out_scale=out_scale),
        grid=(B, DA_HEADS, n_q // tq),
        in_specs=[
            pl.BlockSpec(memory_space=pltpu.SMEM),
            pl.BlockSpec((1, tq, 128), lambda b, h, i: (b, i + qoff, h)),
            pl.BlockSpec((1, R, 128), lambda b, h, i: (b, 0, h)),
            pl.BlockSpec((1, R, 128), lambda b, h, i: (b, 0, h)),
            pl.BlockSpec((1, DA_V_DIM), lambda b, h, i: (0, 0)),
        ],
        out_specs=pl.BlockSpec((1, tq, 128), lambda b, h, i: (b, i, h)),
        out_shape=jax.ShapeDtypeStruct((B, n_q, DA_HEADS * DA_V_DIM), F32),
        compiler_params=_cparams(("parallel", "parallel", "parallel")),
    )(lam.reshape(1).astype(F32), q, k, v, g.reshape(1, DA_V_DIM))


def _bmm(a, b):
    return jnp.einsum('hij,hjk->hik', a.astype(BF16), b.astype(BF16), preferred_element_type=F32)


def _bmm_nt(a, b):
    return jnp.einsum('hid,hjd->hij', a.astype(BF16), b.astype(BF16), preferred_element_type=F32)


def _split3(x):
    hi = x.astype(BF16)
    r = x - hi.astype(F32)
    mid = r.astype(BF16)
    return hi, mid, (r - mid.astype(F32)).astype(BF16)


def _cumsum_rows(tri, x):
    return sum(_bmm(tri, t) for t in _split3(x))


def _cumsum_lanes(x, tri):
    return sum(_bmm_nt(t, tri) for t in _split3(x))


def _unit_tri_inverse(a_strict):
    eye = (lax.broadcasted_iota(I32, (CHUNK, CHUNK), 0) == lax.broadcasted_iota(I32, (CHUNK, CHUNK), 1)).astype(F32)
    n = -a_strict
    p = eye[None] + n
    for _ in range(int(math.log2(CHUNK)) - 1):
        n = _bmm(n, n)
        p = p + _bmm(p, n)
    return p


def _scan_masks(d):
    row = lax.broadcasted_iota(I32, (CHUNK, CHUNK), 0)
    col = lax.broadcasted_iota(I32, (CHUNK, CHUNK), 1)
    delta = (row - col) * (1 - 2 * d)
    return delta >= 0, delta > 0


def _chunk_index(d, s, n_ctx_chunks, n_chunks):
    bwd = jnp.where(s < n_ctx_chunks, n_ctx_chunks - 1 - s, n_chunks - 1 - (s - n_ctx_chunks))
    return jnp.where(d == 0, s, bwd)


def _gdn_kernel(a_ref, d_ref, r_ref, o_ref, s_ref):
    d = pl.program_id(1)

    @pl.when(pl.program_id(2) == 0)
    def _():
        s_ref[...] = jnp.zeros_like(s_ref)

    incl, strict = _scan_masks(d)
    q = a_ref[0, 0, :, 0]
    k = a_ref[0, 1, :, 0]
    v = a_ref[0, 2, :, 0]
    kt = a_ref[0, 3, :, 0]
    gcol = d_ref[0, 0, 0, :, 0]
    bcol = d_ref[0, 0, 1, :, 0]
    grow = r_ref[0, 0, :, 0]
    tri = incl.astype(F32)
    tri_h = jnp.broadcast_to(tri[None], (GDN_HEADS, CHUNK, CHUNK))
    gc_col = _cumsum_rows(tri_h, gcol)
    gc_row = _cumsum_lanes(grow, tri_h)
    decay = jnp.exp(jnp.where(incl[None], gc_col - gc_row, -jnp.inf))
    kb = k * bcol
    a_mat = jnp.where(strict[None], _bmm_nt(kb, k) * decay, 0.0)
    t_inv = _unit_tri_inverse(a_mat)
    u = _bmm(t_inv, v * bcol)
    w = _bmm(t_inv, kb * jnp.exp(gc_col))
    qk = _bmm_nt(q, k) * decay
    q_dec = q * jnp.exp(gc_col)
    g_last = jnp.min(gc_col, axis=1, keepdims=True)
    g_last_row = jnp.min(gc_row, axis=2, keepdims=True)
    kt_dec = kt * jnp.exp(g_last_row - gc_row)
    s = s_ref[...]
    v_new = u - _bmm(w, s)
    o_ref[0, 0, :, 0] = _bmm(q_dec, s) + _bmm(qk, v_new)
    s_ref[...] = s * jnp.exp(g_last) + _bmm(kt_dec, v_new)


def _gdn_scan(a6, d6, rw, n_ctx_chunks):
    B, _, H, NC, _, _ = a6.shape
    cidx = functools.partial(_chunk_index, n_ctx_chunks=n_ctx_chunks, n_chunks=NC)
    return pl.pallas_call(
        _gdn_kernel,
        grid=(B, N_DIRS, NC),
        in_specs=[
            pl.BlockSpec((1, 4, H, 1, CHUNK, CHUNK), lambda b, d, s: (b, 0, 0, cidx(d, s), 0, 0)),
            pl.BlockSpec((1, 1, 2, H, 1, CHUNK, CHUNK), lambda b, d, s: (b, d, 0, 0, cidx(d, s), 0, 0)),
            pl.BlockSpec((1, 1, H, 1, 1, CHUNK), lambda b, d, s: (b, d, 0, cidx(d, s), 0, 0)),
        ],
        out_specs=pl.BlockSpec((1, 1, H, 1, CHUNK, CHUNK), lambda b, d, s: (b, d, 0, cidx(d, s), 0, 0)),
        out_shape=jax.ShapeDtypeStruct((B, N_DIRS, H, NC, CHUNK, CHUNK), F32),
        scratch_shapes=[pltpu.VMEM((H, CHUNK, CHUNK), F32)],
        compiler_params=_cparams(("parallel", "parallel", "arbitrary")),
    )(a6, d6, rw)


def _rwkv_kernel(x_ref, y_ref, o_ref, s_ref):
    d = pl.program_id(1)

    @pl.when(pl.program_id(2) == 0)
    def _():
        s_ref[...] = jnp.zeros_like(s_ref)

    incl, strict = _scan_masks(d)
    r = x_ref[0, 0, :, 0]
    v = x_ref[0, 1, :, 0]
    kk = x_ref[0, 2, :, 0]
    kkt = x_ref[0, 3, :, 0]
    k = y_ref[0, 0, 0, :, 0]
    a = y_ref[0, 0, 1, :, 0]
    lw = y_ref[0, 0, 2, :, 0]
    kt = y_ref[0, 0, 3, :, 0]
    at = y_ref[0, 0, 4, :, 0]
    lwt = y_ref[0, 0, 5, :, 0]
    tri = incl.astype(F32)
    tri_h = jnp.broadcast_to(tri[None], (RWKV_HEADS, CHUNK, CHUNK))
    cum = _cumsum_rows(tri_h, lw)
    cum_t = _cumsum_lanes(lwt, tri_h)
    inv = jnp.exp(-cum)
    b_s = kk * a * inv
    k_s = k * inv
    c_s = kk * jnp.exp(cum - lw)
    r_s = r * jnp.exp(cum)
    l_cb = jnp.where(strict[None], _bmm_nt(c_s, b_s), 0.0)
    l_ck = jnp.where(strict[None], _bmm_nt(c_s, k_s), 0.0)
    a_rk = jnp.where(incl[None], _bmm_nt(r_s, k_s), 0.0)
    a_rb = jnp.where(incl[None], _bmm_nt(r_s, b_s), 0.0)
    t_inv = _unit_tri_inverse(l_cb)
    u1 = _bmm(t_inv, _bmm(l_ck, v))
    cw = _bmm(t_inv, c_s)
    y1 = _bmm(a_rk, v)
    last_t = jnp.min(cum_t, axis=2, keepdims=True)
    kt_end = kt * jnp.exp(last_t - cum_t)
    bt_end = kkt * at * jnp.exp(last_t - cum_t)
    s = s_ref[...]
    u = u1 + _bmm(cw, s)
    o_ref[0, 0, :, 0] = y1 + _bmm(r_s, s) - _bmm(a_rb, u)
    s_ref[...] = s * jnp.exp(last_t) + _bmm(kt_end, v) - _bmm(bt_end, u)


def _rwkv_scan(x6, y6, n_ctx_chunks):
    B, _, H, NC, _, _ = x6.shape
    cidx = functools.partial(_chunk_index, n_ctx_chunks=n_ctx_chunks, n_chunks=NC)
    return pl.pallas_call(
        _rwkv_kernel,
        grid=(B, N_DIRS, NC),
        in_specs=[
            pl.BlockSpec((1, 4, H, 1, CHUNK, CHUNK), lambda b, d, s: (b, 0, 0, cidx(d, s), 0, 0)),
            pl.BlockSpec((1, 1, 6, H, 1, CHUNK, CHUNK), lambda b, d, s: (b, d, 0, 0, cidx(d, s), 0, 0)),
        ],
        out_specs=pl.BlockSpec((1, 1, H, 1, CHUNK, CHUNK), lambda b, d, s: (b, d, 0, cidx(d, s), 0, 0)),
        out_shape=jax.ShapeDtypeStruct((B, N_DIRS, H, NC, CHUNK, CHUNK), F32),
        scratch_shapes=[pltpu.VMEM((H, CHUNK, CHUNK), F32)],
        compiler_params=_cparams(("parallel", "parallel", "arbitrary")),
    )(x6, y6)


def _outproj_kernel(a_ref, g_ref, r_ref, x_ref, w_ref, ng_ref, mods_ref, o_ref, *, tm, n_ctx):
    row0 = pl.program_id(1) * tm
    na, ng = a_ref.shape[2], g_ref.shape[2]
    mix = jnp.dot(a_ref[0].astype(BF16), w_ref[0:na, :], preferred_element_type=F32)
    mix += jnp.dot(g_ref[0].astype(BF16), w_ref[na:na + ng, :], preferred_element_type=F32)
    mix += jnp.dot(r_ref[0].astype(BF16), w_ref[na + ng:, :], preferred_element_type=F32)
    gate = _row_mod(mods_ref, 2, row0, tm, n_ctx)
    o_ref[0] = x_ref[0] + gate * _rms(mix, ng_ref[...])


def _outproj(a, g, r, xa, w, norm_g, mods, n_ctx):
    B, R, D = xa.shape
    tm = _pick(R, (768, 512, 256, 128, 64))
    row = lambda n: pl.BlockSpec((1, tm, n), lambda b, i: (b, i, 0))
    return pl.pallas_call(
        functools.partial(_outproj_kernel, tm=tm, n_ctx=n_ctx),
        grid=(B, R // tm),
        in_specs=[row(a.shape[2]), row(g.shape[2]), row(r.shape[2]), row(D),
                  pl.BlockSpec(w.shape, lambda b, i: (0, 0)),
                  pl.BlockSpec((1, D), lambda b, i: (0, 0)),
                  pl.BlockSpec((1, 2, 6, D), lambda b, i: (b, 0, 0, 0))],
        out_specs=row(D),
        out_shape=jax.ShapeDtypeStruct((B, R, D), F32),
        compiler_params=_cparams(("parallel", "parallel")),
    )(a, g, r, xa, w, norm_g.reshape(1, D), mods)


def _router_kernel(x_ref, g_ref, mods_ref, wr_ref, h_ref, aff_ref, *, tm, n_ctx):
    row0 = pl.program_id(1) * tm
    h = _rms(x_ref[0], g_ref[...])
    h = h * (1.0 + _row_mod(mods_ref, 4, row0, tm, n_ctx)) + _row_mod(mods_ref, 3, row0, tm, n_ctx)
    h_ref[0] = h.astype(BF16)
    logits = jnp.dot(h, wr_ref[...], precision=HI, preferred_element_type=F32)
    e = jnp.exp(logits - jnp.max(logits, axis=-1, keepdims=True))
    aff_ref[0] = e / jnp.sum(e, axis=-1, keepdims=True)


def _router(xa, g, mods, wr, n_ctx):
    B, R, D = xa.shape
    E = wr.shape[1]
    tm = _pick(R, (768, 512, 256, 128, 64))
    return pl.pallas_call(
        functools.partial(_router_kernel, tm=tm, n_ctx=n_ctx),
        grid=(B, R // tm),
        in_specs=[pl.BlockSpec((1, tm, D), lambda b, i: (b, i, 0)),
                  pl.BlockSpec((1, D), lambda b, i: (0, 0)),
                  pl.BlockSpec((1, 2, 6, D), lambda b, i: (b, 0, 0, 0)),
                  pl.BlockSpec((D, E), lambda b, i: (0, 0))],
        out_specs=[pl.BlockSpec((1, tm, D), lambda b, i: (b, i, 0)),
                   pl.BlockSpec((1, tm, E), lambda b, i: (b, i, 0))],
        out_shape=[jax.ShapeDtypeStruct((B, R, D), BF16), jax.ShapeDtypeStruct((B, R, E), F32)],
        compiler_params=_cparams(("parallel", "parallel")),
    )(xa, g.reshape(1, D), mods, wr)


def _select_kernel(a_ref, rank_ref, rsel_ref, *, cap, n_rows):
    a = a_ref[0]
    rows = a.shape[0]
    ri =lax.broadcasted_iota(I32, (rows, rows), 0)
    ci = lax.broadcasted_iota(I32, (rows, rows), 1)
    shift = int(math.log2(n_rows))
    same = (ri >> shift) == (ci >> shift)
    same_e = jnp.where(same, 1.0, 0.0).astype(BF16)
    before_e = jnp.where(same & (ci < ri), 1.0, 0.0).astype(BF16)
    li = lax.broadcasted_iota(I32, (LANES, LANES), 0)
    lj = lax.broadcasted_iota(I32, (LANES, LANES), 1)
    upper = jnp.where(li <= lj, 1.0, 0.0).astype(BF16)
    ones = jnp.ones((LANES, LANES), BF16)

    def row_tot(m):
        return jnp.dot(m, ones, preferred_element_type=F32).astype(BF16)

    def count(mask):
        return jnp.dot(same_e, row_tot(jnp.where(mask, 1.0, 0.0).astype(BF16)), preferred_element_type=F32)

    def prefix(mask):
        m = jnp.where(mask, 1.0, 0.0).astype(BF16)
        incl = jnp.dot(m, upper, preferred_element_type=F32)
        off = jnp.dot(before_e, row_tot(m), preferred_element_type=F32)
        return incl - m.astype(F32) + off

    def body(it, tau):
        cand = tau | jnp.left_shift(jnp.int32(1), 30 - it)
        return jnp.where(count(a >= pltpu.bitcast(cand, F32)) >= cap, cand, tau)

    tau = lax.fori_loop(0, 31, body, jnp.zeros(a.shape, I32))
    sure = a >= pltpu.bitcast(tau + 1, F32)
    band = jnp.logical_and(a >= pltpu.bitcast(tau, F32), jnp.logical_not(sure))
    need = cap - count(sure)
    sel = jnp.logical_or(sure, jnp.logical_and(band, prefix(band) < need))
    rank = prefix(sel).astype(I32)
    rank_ref[0] = rank
    rsel_ref[0] = jnp.where(sel, rank, -1)


def _select(aff_et, cap):
    B, E, T = aff_et.shape
    n_rows = T // LANES
    assert n_rows * LANES == T and n_rows & (n_rows - 1) == 0
    shp = (B, E * n_rows, LANES)
    spec = pl.BlockSpec((1, E * n_rows, LANES), lambda b: (b, 0, 0))
    rank, rsel = pl.pallas_call(
        functools.partial(_select_kernel, cap=cap, n_rows=n_rows),
        grid=(B,),
        in_specs=[spec],
        out_specs=[spec, spec],
        out_shape=[jax.ShapeDtypeStruct(shp, I32), jax.ShapeDtypeStruct(shp, I32)],
        compiler_params=_cparams(("parallel",)),
    )(aff_et.reshape(shp))
    return rank.reshape(B, E, T), rsel.reshape(B, E, T)


def _tile_chunks(cnt_ref, base, j, cs, nk):
    lo = cnt_ref[base + j]
    hi = cnt_ref[base + j + 1]
    k0 = jnp.minimum(lo // cs, nk - 1)
    k1 = jnp.minimum(jnp.maximum(hi - 1, lo) // cs, nk - 1)
    return lo, hi, k0, k1


def _moe_ffn_kernel(cnt_ref, h_ref, rs_ref, wg_ref, wu_ref, wd_ref, ye_ref, xacc_ref, *, nt, tt, cs, nk):
    b, e, j = pl.program_id(0), pl.program_id(1), pl.program_id(2)
    n_e = pl.num_programs(1)

    @pl.when(j == 0)
    def _():
        xacc_ref[...] = jnp.zeros_like(xacc_ref)

    lo, hi, k0, k1 = _tile_chunks(cnt_ref, (b * n_e + e) * (nt + 1), j, cs, nk)
    rs = rs_ref[0, 0]
    slot = lax.broadcasted_iota(I32, (cs, tt), 0)

    def gather(kc):
        onehot = jnp.where(rs == slot + kc * cs, 1.0, 0.0).astype(BF16)
        start = pl.multiple_of(kc * cs, cs)
        xacc_ref[pl.ds(start, cs), :] += jnp.dot(onehot, h_ref[0], preferred_element_type=F32)

    @pl.when(hi > lo)
    def _():
        gather(k0)

        @pl.when(k1 > k0)
        def _():
            gather(k1)

    @pl.when(j == nt - 1)
    def _():
        xe = xacc_ref[...].astype(BF16)
        gt = jnp.dot(xe, wg_ref[0], preferred_element_type=F32)
        up = jnp.dot(xe, wu_ref[0], preferred_element_type=F32)
        hid = (gt * jax.nn.sigmoid(gt) * up).astype(BF16)
        ye_ref[0, 0] = jnp.dot(hid, wd_ref[0], preferred_element_type=F32).astype(BF16)


def _moe_ffn(cnt, h, rsel_et, wg, wu, wd, *, row0, n_tok, cap, tt, cs):
    B, R, D = h.shape
    E, _, F = wg.shape
    nt, nk = n_tok // tt, cap // cs
    assert (nt == 1 and nk == 1) or tt <= cs
    off = row0 // tt
    assert off * tt == row0
    grid_spec = pltpu.PrefetchScalarGridSpec(
        num_scalar_prefetch=1,
        grid=(B, E, nt),
        in_specs=[
            pl.BlockSpec((1, tt, D), lambda b, e, j, c: (b, j + off, 0)),
            pl.BlockSpec((1, 1, 1, tt), lambda b, e, j, c: (b, e, 0, j)),
            pl.BlockSpec((1, D, F), lambda b, e, j, c: (e, 0, 0)),
            pl.BlockSpec((1, D, F), lambda b, e, j, c: (e, 0, 0)),
            pl.BlockSpec((1, F, D), lambda b, e, j, c: (e, 0, 0)),
        ],
        out_specs=pl.BlockSpec((1, 1, cap, D), lambda b, e, j, c: (b, e, 0, 0)),
        scratch_shapes=[pltpu.VMEM((cap, D), F32)],
    )
    return pl.pallas_call(
        functools.partial(_moe_ffn_kernel, nt=nt, tt=tt, cs=cs, nk=nk),
        grid_spec=grid_spec,
        out_shape=jax.ShapeDtypeStruct((B, E, cap, D), BF16),
        compiler_params=_cparams(("parallel", "parallel", "arbitrary")),
    )(cnt, h, rsel_et.reshape(B, E, 1, n_tok), wg, wu, wd)


def _moe_scatter_kernel(cnt_ref, *refs, nt, tt, cs, nk, mod_row, n_e):
    ye_refs = refs[:2 * n_e]
    rs_ref, aff_ref, x_ref, ng_ref, mods_ref, o_ref, yacc_ref = refs[2 * n_e:]
    b, j = pl.program_id(0), pl.program_id(1)
    yacc_ref[...] = jnp.zeros_like(yacc_ref)
    slot0 = lax.broadcasted_iota(I32, (tt, cs), 1)

    for e in range(n_e):
        lo, hi, k0, k1 = _tile_chunks(cnt_ref, (b * n_e + e) * (nt + 1), j, cs, nk)

        def add(ye_ref, kc, e=e):
            rcol = rs_ref[0, :, e:e + 1]
            onehot = jnp.where(rcol == slot0 + kc * cs, 1.0, 0.0).astype(BF16)
            yacc_ref[...] += aff_ref[0, :, e:e + 1] * jnp.dot(onehot, ye_ref[0, 0], preferred_element_type=F32)

        @pl.when(hi > lo)
        def _():
            add(ye_refs[2 * e], k0)

        @pl.when((hi > lo) & (k1 > k0))
        def _():
            add(ye_refs[2 * e + 1], k1)

    gate = mods_ref[0, mod_row, 5:6, :]
    o_ref[0] = x_ref[0] + gate * _rms(yacc_ref[...], ng_ref[...])


def _moe_scatter(cnt, ye, rsel_te, aff_te, xa, norm_g, mods, *, row0, n_tok, cap, tt, cs, mod_row):
    B, R, D = xa.shape
    E = ye.shape[1]
    nt, nk = n_tok // tt, cap // cs
    off = row0 // tt

    def ye_map(b, j, c, *, e, second):
        _, _, k0, k1 = _tile_chunks(c, (b * E + e) * (nt + 1), j, cs, nk)
        return (b, e, k1 if second else k0, 0)

    ye_specs = [pl.BlockSpec((1, 1, cs, D), functools.partial(ye_map, e=e, second=second))
                for e in range(E) for second in (False, True)]
    tok = lambda n: pl.BlockSpec((1, tt, n), lambda b, j, c: (b, j + off, 0))
    grid_spec = pltpu.PrefetchScalarGridSpec(
        num_scalar_prefetch=1,
        grid=(B, nt),
        in_specs=ye_specs + [
            tok(E), tok(E), tok(D),
            pl.BlockSpec((1, D), lambda b, j, c: (0, 0)),
            pl.BlockSpec((1, 2, 6, D), lambda b, j, c: (b, 0, 0, 0)),
        ],
        out_specs=tok(D),
        scratch_shapes=[pltpu.VMEM((tt, D), F32)],
    )
    return pl.pallas_call(
        functools.partial(_moe_scatter_kernel, nt=nt, tt=tt, cs=cs, nk=nk, mod_row=mod_row, n_e=E),
        grid_spec=grid_spec,
        out_shape=jax.ShapeDtypeStruct((B, R, D), F32),
        input_output_aliases={2 * E + 3: 0},
        compiler_params=_cparams(("parallel", "parallel")),
    )(cnt, *([ye] * (2 * E)), rsel_te, aff_te, xa, norm_g.reshape(1, D), mods)


def _moe(xa, h, aff, wg, wu, wd, norm_g, mods, *, row0, n_tok, mod_row):
    B, R, D = xa.shape
    E = aff.shape[2]
    cap = EC_CAPACITY_FACTOR * n_tok // E
    if n_tok <= 256:
        tt, cs = n_tok, cap
    else:
        tt = cs = min(256, cap)
    aff_et = jnp.swapaxes(aff[:, row0:row0 + n_tok], 1, 2)
    rank, rsel = _select(aff_et, cap)
    cnt = jnp.concatenate([rank[:, :, ::tt], jnp.full((B, E, 1), cap, I32)], axis=2).reshape(-1)
    ye = _moe_ffn(cnt, h, rsel, wg, wu, wd, row0=row0, n_tok=n_tok, cap=cap, tt=tt, cs=cs)
    rsel_te = jnp.pad(jnp.swapaxes(rsel, 1, 2), ((0, 0), (row0, R - row0 - n_tok), (0, 0)), constant_values=-1)
    return _moe_scatter(cnt, ye, rsel_te, aff, xa, norm_g, mods,
                        row0=row0, n_tok=n_tok, cap=cap, tt=tt, cs=cs, mod_row=mod_row)


def _rope_tables(rows):
    row = jnp.repeat(jnp.arange(rows, dtype=F32), GRID_W)
    col = jnp.tile(jnp.arange(GRID_W, dtype=F32), rows)
    freqs = ROPE_BASE ** (-jnp.arange(ROPE_FREQS, dtype=F32) / ROPE_FREQS)
    ang = jnp.stack([row[:, None] * freqs, col[:, None] * freqs], axis=1)
    return jnp.cos(ang), jnp.sin(ang)


def _rope(x, cos, sin):
    shp = x.shape
    xs = x.reshape(shp[:-1] + (2, 2, ROPE_FREQS))
    x1, x2 = xs[..., 0, :], xs[..., 1, :]
    cb, sb = cos[None, :, None, None], sin[None, :, None, None]
    return jnp.stack([x1 * cb - x2 * sb, x2 * cb + x1 * sb], axis=-2).reshape(shp)


def _to_chunks(t):
    B, R, H, X = t.shape
    return jnp.moveaxis(t.reshape(B, R // CHUNK, CHUNK, H, X), 3, 1)


def _from_chunks(t):
    B, H, NC, C, X = t.shape
    return jnp.moveaxis(t, 1, 3).reshape(B, NC * C, H, X)


def _l2norm(x):
    return x * lax.rsqrt(jnp.sum(x * x, axis=-1, keepdims=True) + 1e-12)


def _short_conv_silu(x, w, n_ctx):
    pad = GDN_CONV // 2

    def conv(seg):
        sp = jnp.pad(seg, ((0, 0), (pad, pad), (0, 0)))
        n = seg.shape[1]
        return sum(sp[:, i:i + n] * w[i] for i in range(GDN_CONV))

    return jax.nn.silu(jnp.concatenate([conv(x[:, :n_ctx]), conv(x[:, n_ctx:])], axis=1))


def _gdn_mixer(qkv, gate, a_in, b_in, conv_w, a_log, dt_bias, norm_g, n_ctx):
    B, R, _ = qkv.shape
    H, Dh = GDN_HEADS, GDN_HEAD_DIM
    y = _short_conv_silu(qkv, conv_w, n_ctx).reshape(B, R, 3, H, Dh)
    q = _l2norm(y[:, :, 0]) * Dh ** -0.5
    k = _l2norm(y[:, :, 1])
    v = y[:, :, 2]
    g = -jnp.exp(a_log) * jax.nn.softplus(a_in.reshape(B, R, N_DIRS, H) + dt_bias)
    beta = jax.nn.sigmoid(b_in.reshape(B, R, N_DIRS, H))
    qc, kc, vc = _to_chunks(q), _to_chunks(k), _to_chunks(v)
    a6 = jnp.stack([qc, kc, vc, jnp.swapaxes(kc, -1, -2)], axis=1)
    gd = jnp.moveaxis(g, 2, 1)
    bd = jnp.moveaxis(beta, 2, 1)
    expand = lambda t: jnp.broadcast_to(
        _to_chunks(t.reshape(B * N_DIRS, R, H, 1)), (B * N_DIRS, H, R // CHUNK, CHUNK, Dh)
    ).reshape(B, N_DIRS, H, R // CHUNK, CHUNK, Dh)
    d6 = jnp.stack([expand(gd), expand(bd)], axis=2)
    rw = jnp.moveaxis(gd, 3, 2).reshape(B, N_DIRS, H, R // CHUNK, 1, CHUNK)
    o = _gdn_scan(a6, d6, rw, n_ctx // CHUNK)
    o = _from_chunks(o[:, 0] + o[:, 1])
    y = _rms(o, norm_g) * jax.nn.silu(gate.reshape(B, R, H, Dh))
    return y.reshape(B, R, GDN_WIDTH)


def _rwkv_mixer(rkv, xw, xa, xg, w0, w_up, a0, a_up, g_up, k_k, k_a, r_k, gn_w, gn_b):
    B, R, _ = rkv.shape
    H, Dh, W = RWKV_HEADS, RWKV_HEAD_DIM, RWKV_WIDTH
    heads = lambda t: t.reshape(B, R, H, Dh)
    r, k, v = rkv[..., :W], rkv[..., W:2 * W], rkv[..., 2 * W:]
    kk = _l2norm(heads(k * k_k))
    g = jnp.dot(jax.nn.sigmoid(xg), g_up, precision=HI)
    per_dir = []
    kd_sum = 0.0
    for d in range(N_DIRS):
        logw = -RWKV_DECAY_SCALE * jax.nn.sigmoid(w0[d] + jnp.dot(jnp.tanh(xw), w_up[d], precision=HI))
        a_d = jax.nn.sigmoid(a0[d] + jnp.dot(xa, a_up[d], precision=HI))
        kd = k * (1.0 + (a_d - 1.0) * k_a)
        kd_sum = kd_sum + kd
        kc, ac, lc = _to_chunks(heads(kd)), _to_chunks(heads(a_d)), _to_chunks(heads(logw))
        tr = lambda t: jnp.swapaxes(t, -1, -2)
        per_dir.append(jnp.stack([kc, ac, lc, tr(kc), tr(ac), tr(lc)], axis=1))
    y6 = jnp.stack(per_dir, axis=1)
    kkc = _to_chunks(kk)
    x6 = jnp.stack([_to_chunks(heads(r)), _to_chunks(heads(v)), kkc, jnp.swapaxes(kkc, -1, -2)], axis=1)
    return x6, y6, heads(r), heads(v), heads(kd_sum), g


def _rwkv_finish(o, r, v, kd_sum, g, r_k, gn_w, gn_b):
    y = _from_chunks(o[:, 0] + o[:, 1])
    B, R = y.shape[:2]
    mu = jnp.mean(y, axis=-1, keepdims=True)
    var = jnp.mean(jnp.square(y - mu), axis=-1, keepdims=True)
    yn = ((y - mu) * lax.rsqrt(var + RWKV_GN_EPS)).reshape(B, R, RWKV_WIDTH) * gn_w + gn_b
    bonus = (jnp.sum(r * kd_sum * r_k, axis=-1, keepdims=True) * v).reshape(B, R, RWKV_WIDTH)
    return (yn + bonus) * g


def kernel(x, c, ctx, c_ctx, w_mod, b_mod, norm_pre, norm_post, w_in, w_out, da_lambda, da_norm, gdn_conv,
           gdn_a_log, gdn_dt_bias, gdn_norm, rwkv_w0, rwkv_w_up, rwkv_a0, rwkv_a_up, rwkv_g_up, rwkv_k_k,
           rwkv_k_a, rwkv_r_k, rwkv_gn_w, rwkv_gn_b, moe_router, moe_w_gate, moe_w_up, moe_w_down):
    B, T, D = x.shape
    n_ctx = ctx.shape[1]
    R = n_ctx + T
    depth = w_mod.shape[0]
    cos, sin = _rope_tables(T // GRID_W)
    cond = jnp.concatenate([jax.nn.silu(c_ctx.astype(F32))[None], jax.nn.silu(c.astype(F32))], axis=0)
    xa = jnp.concatenate([ctx.astype(x.dtype), x], axis=1)
    n_pad = -D_IN % LANES
    offs = [0]
    for n in IN_SIZES:
        offs.append(offs[-1] + n)

    for i in range(depth):
        last = i == depth - 1
        m = jnp.dot(cond, w_mod[i], precision=HI) + b_mod[i]
        mods = jnp.stack([jnp.broadcast_to(m[:1], (B, 6 * D)), m[1:]], axis=1).reshape(B, 2, 6, D)

        w_in_p = jnp.pad(w_in[i], ((0, 0), (0, n_pad))).astype(BF16)
        p = _inproj(xa, norm_pre[i, 0], mods, w_in_p, n_ctx)
        parts = [p[..., offs[j]:offs[j + 1]] for j in range(len(IN_SIZES))]

        lam_init = 0.8 - 0.6 * math.exp(-0.3 * i)
        lv = da_lambda[i].astype(F32)
        lam = jnp.exp(jnp.sum(lv[0] * lv[1])) - jnp.exp(jnp.sum(lv[2] * lv[3])) + lam_init
        q5 = parts[0].reshape(B, R, DA_HEADS, 2, DA_QK_DIM)
        k5 = parts[1].reshape(B, R, DA_HEADS, 2, DA_QK_DIM)
        q5 = jnp.concatenate([q5[:, :n_ctx], _rope(q5[:, n_ctx:], cos, sin)], axis=1) * DA_QK_DIM ** -0.5
        k5 = jnp.concatenate([k5[:, :n_ctx], _rope(k5[:, n_ctx:], cos, sin)], axis=1)
        qa = q5.reshape(B, R, -1).astype(BF16)
        ka = k5.reshape(B, R, -1).astype(BF16)
        va = parts[2].astype(BF16)
        attn = functools.partial(_attention, qa, ka, va, lam, da_norm[i], out_scale=1.0 - lam_init)
        a_l = attn(q_row0=n_ctx, n_q=T, n_kv_rows=R)
        a_c = jnp.zeros((B, n_ctx, a_l.shape[2]), F32) if last else attn(q_row0=0, n_q=n_ctx, n_kv_rows=n_ctx)
        a_out = jnp.concatenate([a_c, a_l], axis=1)

        g_out = _gdn_mixer(parts[3], parts[4], parts[5], parts[6], gdn_conv[i], gdn_a_log[i],
                           gdn_dt_bias[i], gdn_norm[i], n_ctx)

        x6, y6, r_h, v_h, kd_sum, g_r = _rwkv_mixer(
            parts[7], parts[8], parts[9], parts[10], rwkv_w0[i], rwkv_w_up[i], rwkv_a0[i], rwkv_a_up[i],
            rwkv_g_up[i], rwkv_k_k[i], rwkv_k_a[i], rwkv_r_k[i], rwkv_gn_w[i], rwkv_gn_b[i])
        o_r = _rwkv_scan(x6, y6, n_ctx // CHUNK)
        r_out = _rwkv_finish(o_r, r_h, v_h, kd_sum, g_r, rwkv_r_k[i], rwkv_gn_w[i], rwkv_gn_b[i])

        xa = _outproj(a_out, g_out, r_out, xa, w_out[i].astype(BF16), norm_post[i, 0], mods, n_ctx)

        h, aff = _router(xa, norm_pre[i, 1], mods, moe_router[i], n_ctx)
        wg, wu, wd = moe_w_gate[i].astype(BF16), moe_w_up[i].astype(BF16), moe_w_down[i].astype(BF16)
        xa = _moe(xa, h, aff, wg, wu, wd, norm_post[i, 1], mods, row0=n_ctx, n_tok=T, mod_row=1)
        if not last:
            xa = _moe(xa, h, aff, wg, wu, wd, norm_post[i, 1], mods, row0=0, n_tok=n_ctx, mod_row=0)
    return xa[:, n_ctx:]
```

```python
import functools
import math

import jax
import jax.numpy as jnp
from jax import lax
from jax.experimental import pallas as pl
from jax.experimental.pallas import tpu as pltpu

F32 = jnp.float32
BF16 = jnp.bfloat16
I32 = jnp.int32
HI = lax.Precision.HIGHEST

GRID_W = 64
NORM_EPS = 1e-6
DA_HEADS = 4
DA_QK_DIM = 64
DA_V_DIM = 128
DA_WIDTH = DA_HEADS * DA_V_DIM
ROPE_BASE = 10000.0
ROPE_FREQS = DA_QK_DIM // 4
HEADS = 4
HEAD_DIM = 64
WIDTH = HEADS * HEAD_DIM
GDN_CONV = 5
CHUNK = 64
RWKV_W_LORA = 32
RWKV_A_LORA = 32
RWKV_G_LORA = 64
RWKV_DECAY_SCALE = 0.6065306597126334
RWKV_GN_EPS = 64e-5
N_EXPERTS = 16
EC_CAPACITY_FACTOR = 2
N_DIRS = 2
IN_SIZES = (512, 512, 512, 768, 256, 8, 8, 768, 32, 32, 64)
LANES = 128
SUBLANES = 8
VMEM_LIMIT = 56 * 1024 * 1024

W_ROPE = 4 * 512
W_V = 512
W_REST = 2048
M_A, M_B, M_XW, M_XA, M_XG = 0, 8, 16, 48, 80
GDN_PACK = 7
RWKV_PACK = 11


def _pick(n, cands):
    for c in cands:
        if n % c == 0:
            return c
    raise ValueError(f"no tile for {n} in {cands}")


def _cparams(sem):
    return pltpu.CompilerParams(dimension_semantics=sem, vmem_limit_bytes=VMEM_LIMIT)


def _rms(x, g):
    return x * lax.rsqrt(jnp.mean(x * x, axis=-1, keepdims=True) + NORM_EPS) * g


def _row_mod(mods_ref, idx, row0, n_rows, n_ctx):
    rows = row0 + lax.broadcasted_iota(I32, (n_rows, 1), 0)
    return jnp.where(rows < n_ctx, mods_ref[0, 0, idx:idx + 1, :], mods_ref[0, 1, idx:idx + 1, :])


def _block(i, size):
    return pl.ds(i * size, size) if isinstance(i, int) else pl.ds(pl.multiple_of(i * size, size), size)


def _split3(x):
    hi = x.astype(BF16)
    r = x - hi.astype(F32)
    mid = r.astype(BF16)
    return hi, mid, (r - mid.astype(F32)).astype(BF16)


def _mm_exact_rhs(x, sel):
    return sum(jnp.dot(t, sel, preferred_element_type=F32) for t in _split3(x))


def _mm_hi(a, b):
    ah = a.astype(BF16)
    al = (a - ah.astype(F32)).astype(BF16)
    bh = b.astype(BF16)
    bl = (b - bh.astype(F32)).astype(BF16)
    d = lambda u, v: jnp.dot(u, v, preferred_element_type=F32)
    return d(ah, bh) + d(ah, bl) + d(al, bh)


def _head_ones():
    shift = int(math.log2(HEAD_DIM))
    r = lax.broadcasted_iota(I32, (WIDTH, WIDTH), 0) >> shift
    c = lax.broadcasted_iota(I32, (WIDTH, WIDTH), 1) >> shift
    return r == c


def _inproj_kernel(x_ref, g_ref, mods_ref, w_ref, cos_ref, sin_ref, q_ref, k_ref, v_ref, rest_ref, *, tm, n_ctx):
    row0 = pl.program_id(1) * tm
    h = _rms(x_ref[0], g_ref[...])
    h = h * (1.0 + _row_mod(mods_ref, 1, row0, tm, n_ctx)) + _row_mod(mods_ref, 0, row0, tm, n_ctx)
    hb = h.astype(BF16)
    cos = jnp.concatenate([cos_ref[...]] * 4, axis=1)
    sin = jnp.concatenate([sin_ref[...]] * 4, axis=1)
    proj = lambda a, b: jnp.dot(hb, w_ref[:, a:b], preferred_element_type=F32)
    q_ref[0] = ((proj(0, 512) * cos + proj(512, 1024) * sin) * DA_QK_DIM ** -0.5).astype(BF16)
    k_ref[0] = (proj(1024, 1536) * cos + proj(1536, 2048) * sin).astype(BF16)
    v_ref[0] = proj(W_ROPE, W_ROPE + W_V).astype(BF16)
    rest_ref[0] = proj(W_ROPE + W_V, W_ROPE + W_V + W_REST)


def _inproj(xa, g, mods, w, cos, sin, n_ctx):
    B, R, D = xa.shape
    tm = _pick(R, (384, 256, 128, 64))
    row = lambda n: pl.BlockSpec((1, tm, n), lambda b, i: (b, i, 0))
    return pl.pallas_call(
        functools.partial(_inproj_kernel, tm=tm, n_ctx=n_ctx),
        grid=(B, R // tm),
        in_specs=[
            row(D),
            pl.BlockSpec((1, D), lambda b, i: (0, 0)),
            pl.BlockSpec((1, 2, 6, D), lambda b, i: (b, 0, 0, 0)),
            pl.BlockSpec(w.shape, lambda b, i: (0, 0)),
            pl.BlockSpec((tm, LANES), lambda b, i: (i, 0)),
            pl.BlockSpec((tm, LANES), lambda b, i: (i, 0)),
        ],
        out_specs=[row(512), row(512), row(W_V), row(W_REST)],
        out_shape=[jax.ShapeDtypeStruct((B, R, 512), BF16), jax.ShapeDtypeStruct((B, R, 512), BF16),
                   jax.ShapeDtypeStruct((B, R, W_V), BF16), jax.ShapeDtypeStruct((B, R, W_REST), F32)],
        compiler_params=_cparams(("parallel", "parallel")),
    )(xa, g.reshape(1, D), mods, w, cos, sin)


def _attn_kernel(lam_ref, q_ref, k_ref, v_ref, g_ref, o_ref, *, tq, tk_ctx, tk_all, n_ctx, n_rows, out_scale):
    q = q_ref[0]
    lane = lax.broadcasted_iota(I32, q.shape, 1)
    zero = jnp.zeros_like(q)
    qq = jnp.concatenate([jnp.where(lane < DA_QK_DIM, q, zero),
                          jnp.where(lane >= DA_QK_DIM, q, zero)], axis=0)

    def attend(tk, n_kv):
        def body(c, carry):
            m, l, acc = carry
            k = k_ref[0, _block(c, tk), :]
            v = v_ref[0, _block(c, tk), :]
            s = lax.dot_general(qq, k, (((1,), (1,)), ((), ())), preferred_element_type=F32)
            m_new = jnp.maximum(m, jnp.max(s, axis=-1, keepdims=True))
            alpha = jnp.exp(m - m_new)
            p = jnp.exp(s - m_new)
            l = alpha * l + jnp.sum(p, axis=-1, keepdims=True)
            acc = alpha * acc + jnp.dot(p.astype(BF16), v, preferred_element_type=F32)
            return m_new, l, acc

        m0 = jnp.full((2 * tq, 1), -jnp.inf, F32)
        l0 = jnp.zeros((2 * tq, 1), F32)
        a0 = jnp.zeros((2 * tq, DA_V_DIM), F32)
        _, l, acc = lax.fori_loop(0, n_kv, body, (m0, l0, a0))
        o = acc / l
        o = o[:tq] - lam_ref[0] * o[tq:]
        o_ref[0] = _rms(o, g_ref[...]) * out_scale

    is_ctx = pl.program_id(2) < n_ctx // tq

    @pl.when(is_ctx)
    def _():
        attend(tk_ctx, n_ctx // tk_ctx)

    @pl.when(jnp.logical_not(is_ctx))
    def _():
        attend(tk_all, n_rows // tk_all)


def _attention(q, k, v, lam, g, *, n_ctx, out_scale):
    B, R, _ = q.shape
    tq = _pick(math.gcd(n_ctx, R), (256, 128))
    tk_ctx = _pick(n_ctx, (768, 512, 256, 128))
    tk_all = _pick(R, (768, 512, 256, 128))
    return pl.pallas_call(
        functools.partial(_attn_kernel, tq=tq, tk_ctx=tk_ctx, tk_all=tk_all, n_ctx=n_ctx, n_rows=R,
                          out_scale=out_scale),
        grid=(B, DA_HEADS, R // tq),
        in_specs=[
            pl.BlockSpec(memory_space=pltpu.SMEM),
            pl.BlockSpec((1, tq, 128), lambda b, h, i: (b, i, h)),
            pl.BlockSpec((1, R, 128), lambda b, h, i: (b, 0, h)),
            pl.BlockSpec((1, R, 128), lambda b, h, i: (b, 0, h)),
            pl.BlockSpec((1, DA_V_DIM), lambda b, h, i: (0, 0)),
        ],
        out_specs=pl.BlockSpec((1, tq, 128), lambda b, h, i: (b, i, h)),
        out_shape=jax.ShapeDtypeStruct((B, R, DA_WIDTH), F32),
        compiler_params=_cparams(("parallel", "parallel", "parallel")),
    )(lam.reshape(1).astype(F32), q, k, v, g.reshape(1, DA_V_DIM))


def _halo_shift(cur, prev8, next8, s, first, last):
    n = cur.shape[0]
    if s == 0:
        return cur
    row = lax.broadcasted_iota(I32, (n, 1), 0)
    rolled = pltpu.roll(cur, (-s) % n, 0)
    if s < 0:
        halo = jnp.where(first, 0.0, pltpu.roll(prev8, -s, 0))
        return jnp.where(row < -s, jnp.concatenate([halo] * (n // SUBLANES), axis=0), rolled)
    halo = jnp.where(last, 0.0, pltpu.roll(next8, SUBLANES - s, 0))
    return jnp.where(row >= n - s, jnp.concatenate([halo] * (n // SUBLANES), axis=0), rolled)


def _gdn_prep_kernel(prev_ref, cur_ref, next_ref, misc_ref, cw_ref, alog_ref, dtb_ref, o_ref, *, tp, n_ctx, n_rows):
    row0 = pl.program_id(1) * tp
    first = (row0 == 0) | (row0 == n_ctx)
    last = (row0 + tp == n_ctx) | (row0 + tp == n_rows)
    cur, prev8, next8 = cur_ref[0], prev_ref[0], next_ref[0]
    pad = GDN_CONV // 2
    y = sum(_halo_shift(cur, prev8, next8, j - pad, first, last) * cw_ref[j:j + 1, :] for j in range(GDN_CONV))
    y = y * jax.nn.sigmoid(y)
    q, k, v = y[:, :WIDTH], y[:, WIDTH:2 * WIDTH], y[:, 2 * WIDTH:]
    same_head = jnp.where(_head_ones(), 1.0, 0.0).astype(BF16)
    o_ref[0, :, 0:WIDTH] = q * lax.rsqrt(_mm_exact_rhs(q * q, same_head) + 1e-12) * HEAD_DIM ** -0.5
    o_ref[0, :, WIDTH:2 * WIDTH] = k * lax.rsqrt(_mm_exact_rhs(k * k, same_head) + 1e-12)
    o_ref[0, :, 2 * WIDTH:3 * WIDTH] = v
    m = misc_ref[0]
    x = m + dtb_ref[...]
    softplus = jnp.maximum(x, 0.0) + jnp.log1p(jnp.exp(-jnp.abs(x)))
    g_all = -jnp.exp(alog_ref[...]) * softplus
    b_all = jax.nn.sigmoid(m)
    src = lax.broadcasted_iota(I32, (WIDTH, WIDTH), 0)
    head = lax.broadcasted_iota(I32, (WIDTH, WIDTH), 1) >> int(math.log2(HEAD_DIM))
    for d in range(N_DIRS):
        pick_g = jnp.where(src == M_A + d * HEADS + head, 1.0, 0.0).astype(BF16)
        pick_b = jnp.where(src == M_B + d * HEADS + head, 1.0, 0.0).astype(BF16)
        base = (3 + 2 * d) * WIDTH
        o_ref[0, :, base:base + WIDTH] = _mm_exact_rhs(g_all, pick_g)
        o_ref[0, :, base + WIDTH:base + 2 * WIDTH] = _mm_exact_rhs(b_all, pick_b)


def _halo_specs(tp, width, col_block, n_rows):
    r8 = tp // SUBLANES
    n8 = n_rows // SUBLANES
    return [
        pl.BlockSpec((1, SUBLANES, width), lambda b, i: (b, jnp.maximum(i * r8 - 1, 0), col_block)),
        pl.BlockSpec((1, tp, width), lambda b, i: (b, i, col_block)),
        pl.BlockSpec((1, SUBLANES, width), lambda b, i: (b, jnp.minimum((i + 1) * r8, n8 - 1), col_block)),
    ]


def _gdn_prep(rest, conv_w, a_log, dt_bias, n_ctx):
    B, R, _ = rest.shape
    tp = _pick(math.gcd(R, n_ctx), (256, 128, 64))
    alog = jnp.zeros((1, WIDTH), F32).at[0, M_A:M_A + N_DIRS * HEADS].set(a_log.reshape(-1))
    dtb = jnp.zeros((1, WIDTH), F32).at[0, M_A:M_A + N_DIRS * HEADS].set(dt_bias.reshape(-1))
    vec = lambda n: pl.BlockSpec((1, n), lambda b, i: (0, 0))
    return pl.pallas_call(
        functools.partial(_gdn_prep_kernel, tp=tp, n_ctx=n_ctx, n_rows=R),
        grid=(B, R // tp),
        in_specs=_halo_specs(tp, 3 * WIDTH, 0, R) + [
            pl.BlockSpec((1, tp, WIDTH), lambda b, i: (b, i, 7)),
            pl.BlockSpec((GDN_CONV, 3 * WIDTH), lambda b, i: (0, 0)),
            vec(WIDTH), vec(WIDTH)],
        out_specs=pl.BlockSpec((1, tp, GDN_PACK * WIDTH), lambda b, i: (b, i, 0)),
        out_shape=jax.ShapeDtypeStruct((B, R, GDN_PACK * WIDTH), F32),
        compiler_params=_cparams(("parallel", "parallel")),
    )(rest, rest, rest, rest, conv_w, alog, dtb)


def _rwkv_prep_kernel(rkv_ref, misc_ref, wup_ref, vec_ref, o_ref):
    rkv, m = rkv_ref[0], misc_ref[0]
    r, k, v = rkv[:, :WIDTH], rkv[:, WIDTH:2 * WIDTH], rkv[:, 2 * WIDTH:]
    k_k, k_a, r_k = vec_ref[0:1, :], vec_ref[1:2, :], vec_ref[2:3, :]
    same_head = jnp.where(_head_ones(), 1.0, 0.0).astype(BF16)
    kkr = k * k_k
    kk = kkr * lax.rsqrt(_mm_exact_rhs(kkr * kkr, same_head) + 1e-12)
    tanh_m, sig_m = jnp.tanh(m), jax.nn.sigmoid(m)
    o_ref[0, :, 0:WIDTH] = r
    o_ref[0, :, WIDTH:2 * WIDTH] = v
    o_ref[0, :, 2 * WIDTH:3 * WIDTH] = kk
    kd_sum = jnp.zeros_like(k)
    for d in range(N_DIRS):
        w0, a0 = vec_ref[3 + 2 * d:4 + 2 * d, :], vec_ref[4 + 2 * d:5 + 2 * d, :]
        logw = -RWKV_DECAY_SCALE * jax.nn.sigmoid(w0 + _mm_hi(tanh_m, wup_ref[2 * d]))
        a_d = jax.nn.sigmoid(a0 + _mm_hi(m, wup_ref[2 * d + 1]))
        kd = k * (1.0 + (a_d - 1.0) * k_a)
        kd_sum = kd_sum + kd
        base = (3 + 3 * d) * WIDTH
        o_ref[0, :, base:base + WIDTH] = kd
        o_ref[0, :, base + WIDTH:base + 2 * WIDTH] = kk * a_d
        o_ref[0, :, base + 2 * WIDTH:base + 3 * WIDTH] = logw
    o_ref[0, :, 9 * WIDTH:10 * WIDTH] = _mm_hi(sig_m, wup_ref[2 * N_DIRS])
    o_ref[0, :, 10 * WIDTH:11 * WIDTH] = _mm_exact_rhs(r * kd_sum * r_k, same_head) * v


def _rwkv_prep(rest, w_up, a_up, g_up, w0, a0, k_k, k_a, r_k):
    B, R, _ = rest.shape
    tp = _pick(R, (256, 128, 64))
    embed = lambda w, lane0: jnp.zeros((WIDTH, WIDTH), F32).at[lane0:lane0 + w.shape[0]].set(w)
    wup = jnp.stack([embed(w_up[0], M_XW), embed(a_up[0], M_XA), embed(w_up[1], M_XW), embed(a_up[1], M_XA),
                     embed(g_up, M_XG)])
    vecs = jnp.stack([k_k, k_a, r_k.reshape(-1), w0[0], a0[0], w0[1], a0[1], jnp.zeros_like(k_k)])
    return pl.pallas_call(
        _rwkv_prep_kernel,
        grid=(B, R // tp),
        in_specs=[pl.BlockSpec((1, tp, 3 * WIDTH), lambda b, i: (b, i, 1)),
                  pl.BlockSpec((1, tp, WIDTH), lambda b, i: (b, i, 7)),
                  pl.BlockSpec(wup.shape, lambda b, i: (0, 0, 0)),
                  pl.BlockSpec(vecs.shape, lambda b, i: (0, 0))],
        out_specs=pl.BlockSpec((1, tp, RWKV_PACK * WIDTH), lambda b, i: (b, i, 0)),
        out_shape=jax.ShapeDtypeStruct((B, R, RWKV_PACK * WIDTH), F32),
        compiler_params=_cparams(("parallel", "parallel")),
    )(rest, rest, wup, vecs)


def _bmm(a, b):
    return jnp.einsum('nij,njk->nik', a.astype(BF16), b.astype(BF16), preferred_element_type=F32)


def _bmm_nt(a, b):
    return jnp.einsum('nik,njk->nij', a.astype(BF16), b.astype(BF16), preferred_element_type=F32)


def _bmm_tn(a, b):
    return jnp.einsum('nki,nkj->nij', a.astype(BF16), b.astype(BF16), preferred_element_type=F32)


def _bmm_exact_lhs(sel, x):
    return sum(jnp.einsum('nij,njk->nik', sel, t, preferred_element_type=F32) for t in _split3(x))


def _scan_consts(n_batch):
    n = N_DIRS * n_batch
    wide = (n, CHUNK, HEADS * CHUNK)
    sq = (n, CHUNK, CHUNK)
    it = lambda shp, ax: lax.broadcasted_iota(I32, shp, ax)
    sign = lambda shp: jnp.where(it(shp, 0) < n_batch, 1, -1)
    delta = (it(wide, 1) - (it(wide, 2) & (CHUNK - 1))) * sign(wide)
    dsq = (it(sq, 1) - it(sq, 2)) * sign(sq)
    return dict(incl=delta >= 0, strict=delta > 0, eye=delta == 0,
                tri=jnp.where(dsq >= 0, 1.0, 0.0).astype(BF16), bd=_head_ones()[None])


def _bd(x, c):
    return jnp.where(c['bd'], jnp.concatenate([x] * HEADS, axis=1), 0.0)


def _unit_tri_inverse(a, c):
    n = -a
    p = jnp.where(c['eye'], 1.0, 0.0) + n
    nb = _bd(n, c)
    for _ in range(int(math.log2(CHUNK)) - 1):
        n = _bmm(n, nb)
        nb = _bd(n, c)
        p = p + _bmm(p, nb)
    return p


def _chunk_rev(s, n_ctx_chunks, n_chunks):
    return jnp.where(s < n_ctx_chunks, n_ctx_chunks - 1 - s, n_chunks - 1 - (s - n_ctx_chunks))


def _gdn_step(col, c, s):
    q, k, v, gw, bw = col(0, 0), col(1, 1), col(2, 2), col(3, 5), col(4, 6)
    gc = _bmm_exact_lhs(c['tri'], gw)
    ones = jnp.ones(c['tri'].shape, BF16)
    gr = _bmm_exact_lhs(ones, jnp.where(c['eye'], gc, 0.0))
    decay = jnp.exp(jnp.where(c['incl'], gc - gr, -jnp.inf))
    kb = k * bw
    km = _bd(k, c)
    a_mat = jnp.where(c['strict'], _bmm_nt(kb, km) * decay, 0.0)
    qk = _bmm_nt(q, km) * decay
    t_inv = _unit_tri_inverse(a_mat, c)
    uw = _bmm(t_inv, jnp.concatenate([_bd(v * bw, c), _bd(kb * jnp.exp(gc), c)], axis=2))
    u, w = uw[:, :, :WIDTH], uw[:, :, WIDTH:]
    g_end = jnp.min(gc, axis=1, keepdims=True)
    v_new = u - _bmm(w, s)
    o = _bmm(q * jnp.exp(gc), s) + _bmm(qk, _bd(v_new, c))
    return o, s * jnp.exp(g_end) + jnp.where(c['bd'], _bmm_tn(k * jnp.exp(g_end - gc), v_new), 0.0)


def _rwkv_step(col, c, st):
    r, v, kk, kd, b, lw = col(0, 0), col(1, 1), col(2, 2), col(3, 6), col(4, 7), col(5, 8)
    n4 = HEADS * CHUNK
    cum = _bmm_exact_lhs(c['tri'], lw)
    inv = jnp.exp(-cum)
    c_s = kk * jnp.exp(cum - lw)
    r_s = r * jnp.exp(cum)
    big = _bmm_nt(jnp.concatenate([c_s, r_s], axis=1),
                  jnp.concatenate([_bd(b * inv, c), _bd(kd * inv, c)], axis=1))
    l_cb = jnp.where(c['strict'], big[:, :CHUNK, :n4], 0.0)
    l_ck = jnp.where(c['strict'], big[:, :CHUNK, n4:], 0.0)
    a_rb = jnp.where(c['incl'], big[:, CHUNK:, :n4], 0.0)
    a_rk = jnp.where(c['incl'], big[:, CHUNK:, n4:], 0.0)
    t_inv = _unit_tri_inverse(l_cb, c)
    lv = _bmm(jnp.concatenate([l_ck, a_rk], axis=1), _bd(v, c))
    uc = _bmm(t_inv, jnp.concatenate([_bd(lv[:, :CHUNK], c), _bd(c_s, c)], axis=2))
    u1, cw = uc[:, :, :WIDTH], uc[:, :, WIDTH:]
    end = jnp.min(cum, axis=1, keepdims=True)
    su = _bmm_nt(jnp.concatenate([cw, r_s], axis=1), st)
    u = u1 + su[:, :CHUNK]
    y = lv[:, CHUNK:] + su[:, CHUNK:] - _bmm(a_rb, _bd(u, c))
    e_end = jnp.exp(end - cum)
    upd = _bmm_tn(jnp.concatenate([v, u], axis=1), jnp.concatenate([kd * e_end, -(b * e_end)], axis=1))
    return y, st * jnp.exp(end) + jnp.where(c['bd'], upd, 0.0)


def _scan_kernel(f_ref, b_ref, of_ref, ob_ref, s_ref, *, step):
    @pl.when(pl.program_id(0) == 0)
    def _():
        s_ref[...] = jnp.zeros_like(s_ref)

    n_batch = f_ref.shape[0]
    grp = lambda ref, j: ref[:, :, j * WIDTH:(j + 1) * WIDTH]
    col = lambda jf, jb: jnp.concatenate([grp(f_ref, jf), grp(b_ref, jb)], axis=0)
    o, s_new = step(col, _scan_consts(n_batch), s_ref[...])
    of_ref[...] = o[:n_batch]
    ob_ref[...] = o[n_batch:]
    s_ref[...] = s_new


def _scan(packed, step, n_ctx):
    B, R, P = packed.shape
    nc, ncc = R // CHUNK, n_ctx // CHUNK
    rev = functools.partial(_chunk_rev, n_ctx_chunks=ncc, n_chunks=nc)
    out = jax.ShapeDtypeStruct((B, R, WIDTH), F32)
    return pl.pallas_call(
        functools.partial(_scan_kernel, step=step),
        grid=(nc,),
        in_specs=[pl.BlockSpec((B, CHUNK, P), lambda s: (0, s, 0)),
                  pl.BlockSpec((B, CHUNK, P), lambda s: (0, rev(s), 0))],
        out_specs=[pl.BlockSpec((B, CHUNK, WIDTH), lambda s: (0, s, 0)),
                   pl.BlockSpec((B, CHUNK, WIDTH), lambda s: (0, rev(s), 0))],
        out_shape=[out, out],
        scratch_shapes=[pltpu.VMEM((B * N_DIRS, WIDTH, WIDTH), F32)],
        compiler_params=_cparams(("arbitrary",)),
    )(packed, packed)


def _outproj_kernel(a_ref, gf_ref, gb_ref, gg_ref, rf_ref, rb_ref, rg_ref, rbn_ref, x_ref, w_ref, vec_ref, ng_ref,
                    mods_ref, o_ref, *, tm, n_ctx):
    row0 = pl.program_id(1) * tm
    same_head = jnp.where(_head_ones(), 1.0, 0.0).astype(BF16)
    hmean = lambda t: _mm_exact_rhs(t, same_head) * (1.0 / HEAD_DIM)
    og = gf_ref[0] + gb_ref[0]
    gate = gg_ref[0]
    y_g = og * lax.rsqrt(hmean(og * og) + NORM_EPS) * vec_ref[0:1, :] * (gate * jax.nn.sigmoid(gate))
    yr = rf_ref[0] + rb_ref[0]
    dev = yr - hmean(yr)
    yn = dev * lax.rsqrt(hmean(dev * dev) + RWKV_GN_EPS) * vec_ref[1:2, :] + vec_ref[2:3, :]
    y_r = (yn + rbn_ref[0]) * rg_ref[0]
    mix = jnp.dot(a_ref[0].astype(BF16), w_ref[0:DA_WIDTH, :], preferred_element_type=F32)
    mix += jnp.dot(y_g.astype(BF16), w_ref[DA_WIDTH:DA_WIDTH + WIDTH, :], preferred_element_type=F32)
    mix += jnp.dot(y_r.astype(BF16), w_ref[DA_WIDTH + WIDTH:, :], preferred_element_type=F32)
    o_ref[0] = x_ref[0] + _row_mod(mods_ref, 2, row0, tm, n_ctx) * _rms(mix, ng_ref[...])


def _outproj(a, g_f, g_b, r_f, r_b, rest, rw, xa, w, gdn_norm, gn_w, gn_b, norm_g, mods, n_ctx):
    B, R, D = xa.shape
    tm = _pick(R, (384, 256, 128, 64))
    vecs = jnp.stack([jnp.tile(gdn_norm, HEADS), gn_w, gn_b, jnp.zeros_like(gn_w)] + [jnp.zeros_like(gn_w)] * 4)
    row = lambda n, cb=0: pl.BlockSpec((1, tm, n), lambda b, i: (b, i, cb))
    return pl.pallas_call(
        functools.partial(_outproj_kernel, tm=tm, n_ctx=n_ctx),
        grid=(B, R // tm),
        in_specs=[row(DA_WIDTH), row(WIDTH), row(WIDTH), row(WIDTH, 6), row(WIDTH), row(WIDTH),
                  row(WIDTH, 9), row(WIDTH, 10), row(D),
                  pl.BlockSpec(w.shape, lambda b, i: (0, 0)),
                  pl.BlockSpec(vecs.shape, lambda b, i: (0, 0)),
                  pl.BlockSpec((1, D), lambda b, i: (0, 0)),
                  pl.BlockSpec((1, 2, 6, D), lambda b, i: (b, 0, 0, 0))],
        out_specs=row(D),
        out_shape=jax.ShapeDtypeStruct((B, R, D), F32),
        compiler_params=_cparams(("parallel", "parallel")),
    )(a, g_f, g_b, rest, r_f, r_b, rw, rw, xa, w, vecs, norm_g.reshape(1, D), mods)


def _router_kernel(x_ref, g_ref, mods_ref, wr_ref, h_ref, aff_ref, *, tm, n_ctx):
    row0 = pl.program_id(1) * tm
    h = _rms(x_ref[0], g_ref[...])
    h = h * (1.0 + _row_mod(mods_ref, 4, row0, tm, n_ctx)) + _row_mod(mods_ref, 3, row0, tm, n_ctx)
    h_ref[0] = h.astype(BF16)
    logits = jnp.dot(h, wr_ref[...], precision=HI, preferred_element_type=F32)
    e = jnp.exp(logits - jnp.max(logits, axis=-1, keepdims=True))
    aff_ref[0] = e / jnp.sum(e, axis=-1, keepdims=True)


def _router(xa, g, mods, wr, n_ctx):
    B, R, D = xa.shape
    E = wr.shape[1]
    tm = _pick(R, (768, 512, 256, 128, 64))
    return pl.pallas_call(
        functools.partial(_router_kernel, tm=tm, n_ctx=n_ctx),
        grid=(B, R // tm),
        in_specs=[pl.BlockSpec((1, tm, D), lambda b, i: (b, i, 0)),
                  pl.BlockSpec((1, D), lambda b, i: (0, 0)),
                  pl.BlockSpec((1, 2, 6, D), lambda b, i: (b, 0, 0, 0)),
                  pl.BlockSpec((D, E), lambda b, i: (0, 0))],
        out_specs=[pl.BlockSpec((1, tm, D), lambda b, i: (b, i, 0)),
                   pl.BlockSpec((1, tm, E), lambda b, i: (b, i, 0))],
        out_shape=[jax.ShapeDtypeStruct((B, R, D), BF16), jax.ShapeDtypeStruct((B, R, E), F32)],
        compiler_params=_cparams(("parallel", "parallel")),
    )(xa, g.reshape(1, D), mods, wr)


def _select_kernel(a_ref, rank_ref, rsel_ref, *, cap, n_rows):
    a = a_ref[0]
    rows = a.shape[0]
    ri = lax.broadcasted_iota(I32, (rows, rows), 0)
    ci = lax.broadcasted_iota(I32, (rows, rows), 1)
    shift = int(math.log2(n_rows))
    same = (ri >> shift) == (ci >> shift)
    same_e = jnp.where(same, 1.0, 0.0).astype(BF16)
    before_e = jnp.where(same & (ci < ri), 1.0, 0.0).astype(BF16)
    li = lax.broadcasted_iota(I32, (LANES, LANES), 0)
    lj = lax.broadcasted_iota(I32, (LANES, LANES), 1)
    upper = jnp.where(li <= lj, 1.0, 0.0).astype(BF16)
    ones = jnp.ones((LANES, LANES), BF16)

    def row_tot(m):
        return jnp.dot(m, ones, preferred_element_type=F32).astype(BF16)

    def count(mask):
        return jnp.dot(same_e, row_tot(jnp.where(mask, 1.0, 0.0).astype(BF16)), preferred_element_type=F32)

    def prefix(mask):
        m = jnp.where(mask, 1.0, 0.0).astype(BF16)
        incl = jnp.dot(m, upper, preferred_element_type=F32)
        off = jnp.dot(before_e, row_tot(m), preferred_element_type=F32)
        return incl - m.astype(F32) + off

    def body(it, tau):
        cand = tau | jnp.left_shift(jnp.int32(1), 30 - it)
        return jnp.where(count(a >= pltpu.bitcast(cand, F32)) >= cap, cand, tau)

    tau = lax.fori_loop(0, 31, body, jnp.zeros(a.shape, I32))
    sure = a >= pltpu.bitcast(tau + 1, F32)
    band = jnp.logical_and(a >= pltpu.bitcast(tau, F32), jnp.logical_not(sure))
    need = cap - count(sure)
    sel = jnp.logical_or(sure, jnp.logical_and(band, prefix(band) < need))
    rank = prefix(sel).astype(I32)
    rank_ref[0] = rank
    rsel_ref[0] = jnp.where(sel, rank, -1)


def _select(aff_et, cap):
    B, E, T = aff_et.shape
    n_rows = T // LANES
    assert n_rows * LANES == T and n_rows & (n_rows - 1) == 0
    shp = (B, E * n_rows, LANES)
    spec = pl.BlockSpec((1, E * n_rows, LANES), lambda b: (b, 0, 0))
    rank, rsel = pl.pallas_call(
        functools.partial(_select_kernel, cap=cap, n_rows=n_rows),
        grid=(B,),
        in_specs=[spec],
        out_specs=[spec, spec],
        out_shape=[jax.ShapeDtypeStruct(shp, I32), jax.ShapeDtypeStruct(shp, I32)],
        compiler_params=_cparams(("parallel",)),
    )(aff_et.reshape(shp))
    return rank.reshape(B, E, T), rsel.reshape(B, E, T)


def _tile_chunks(cnt_ref, base, j, cs, nk):
    lo = cnt_ref[base + j]
    hi = cnt_ref[base + j + 1]
    k0 = jnp.minimum(lo // cs, nk - 1)
    k1 = jnp.minimum(jnp.maximum(hi - 1, lo) // cs, nk - 1)
    return lo, hi, k0, k1


def _moe_ffn_kernel(cnt_ref, h_ref, rs_ref, wg_ref, wu_ref, wd_ref, ye_ref, xacc_ref, *, nt, tt, cs, nk):
    b, e, j = pl.program_id(0), pl.program_id(1), pl.program_id(2)
    n_e = pl.num_programs(1)

    @pl.when(j == 0)
    def _():
        xacc_ref[...] = jnp.zeros_like(xacc_ref)

    lo, hi, k0, k1 = _tile_chunks(cnt_ref, (b * n_e + e) * (nt + 1), j, cs, nk)
    rs = rs_ref[0, 0]
    slot = lax.broadcasted_iota(I32, (cs, tt), 0)

    def gather(kc):
        onehot = jnp.where(rs == slot + kc * cs, 1.0, 0.0).astype(BF16)
        start = pl.multiple_of(kc * cs, cs)
        xacc_ref[pl.ds(start, cs), :] += jnp.dot(onehot, h_ref[0], preferred_element_type=F32)

    @pl.when(hi > lo)
    def _():
        gather(k0)

        @pl.when(k1 > k0)
        def _():
            gather(k1)

    @pl.when(j == nt - 1)
    def _():
        xe = xacc_ref[...].astype(BF16)
        gt = jnp.dot(xe, wg_ref[0], preferred_element_type=F32)
        up = jnp.dot(xe, wu_ref[0], preferred_element_type=F32)
        hid = (gt * jax.nn.sigmoid(gt) * up).astype(BF16)
        ye_ref[0, 0] = jnp.dot(hid, wd_ref[0], preferred_element_type=F32).astype(BF16)


def _moe_ffn(cnt, h, rsel_et, wg, wu, wd, *, row0, n_tok, cap, tt, cs):
    B, R, D = h.shape
    E, _, F = wg.shape
    nt, nk = n_tok // tt, cap // cs
    assert (nt == 1 and nk == 1) or tt <= cs
    off = row0 // tt
    assert off * tt == row0
    grid_spec = pltpu.PrefetchScalarGridSpec(
        num_scalar_prefetch=1,
        grid=(B, E, nt),
        in_specs=[
            pl.BlockSpec((1, tt, D), lambda b, e, j, c: (b, j + off, 0)),
            pl.BlockSpec((1, 1, 1, tt), lambda b, e, j, c: (b, e, 0, j)),
            pl.BlockSpec((1, D, F), lambda b, e, j, c: (e, 0, 0)),
            pl.BlockSpec((1, D, F), lambda b, e, j, c: (e, 0, 0)),
            pl.BlockSpec((1, F, D), lambda b, e, j, c: (e, 0, 0)),
        ],
        out_specs=pl.BlockSpec((1, 1, cap, D), lambda b, e, j, c: (b, e, 0, 0)),
        scratch_shapes=[pltpu.VMEM((cap, D), F32)],
    )
    return pl.pallas_call(
        functools.partial(_moe_ffn_kernel, nt=nt, tt=tt, cs=cs, nk=nk),
        grid_spec=grid_spec,
        out_shape=jax.ShapeDtypeStruct((B, E, cap, D), BF16),
        compiler_params=_cparams(("parallel", "parallel", "arbitrary")),
    )(cnt, h, rsel_et.reshape(B, E, 1, n_tok), wg, wu, wd)


def _moe_scatter_kernel(cnt_ref, *refs, nt, tt, cs, nk, mod_row, n_e):
    ye_refs = refs[:2 * n_e]
    rs_ref, aff_ref, x_ref, ng_ref, mods_ref, o_ref, yacc_ref = refs[2 * n_e:]
    b, j = pl.program_id(0), pl.program_id(1)
    yacc_ref[...] = jnp.zeros_like(yacc_ref)
    slot0 = lax.broadcasted_iota(I32, (tt, cs), 1)

    for e in range(n_e):
        lo, hi, k0, k1 = _tile_chunks(cnt_ref, (b * n_e + e) * (nt + 1), j, cs, nk)

        def add(ye_ref, kc, e=e):
            rcol = rs_ref[0, :, e:e + 1]
            onehot = jnp.where(rcol == slot0 + kc * cs, 1.0, 0.0).astype(BF16)
            yacc_ref[...] += aff_ref[0, :, e:e + 1] * jnp.dot(onehot, ye_ref[0, 0], preferred_element_type=F32)

        @pl.when(hi > lo)
        def _():
            add(ye_refs[2 * e], k0)

        @pl.when((hi > lo) & (k1 > k0))
        def _():
            add(ye_refs[2 * e + 1], k1)

    gate = mods_ref[0, mod_row, 5:6, :]
    o_ref[0] = x_ref[0] + gate * _rms(yacc_ref[...], ng_ref[...])


def _moe_scatter(cnt, ye, rsel_te, aff_te, xa, norm_g, mods, *, row0, n_tok, cap, tt, cs, mod_row):
    B, R, D = xa.shape
    E = ye.shape[1]
    nt, nk = n_tok // tt, cap // cs
    off = row0 // tt

    def ye_map(b, j, c, *, e, second):
        _, _, k0, k1 = _tile_chunks(c, (b * E + e) * (nt + 1), j, cs, nk)
        return (b, e, k1 if second else k0, 0)

    ye_specs = [pl.BlockSpec((1, 1, cs, D), functools.partial(ye_map, e=e, second=second))
                for e in range(E) for second in (False, True)]
    tok = lambda n: pl.BlockSpec((1, tt, n), lambda b, j, c: (b, j + off, 0))
    grid_spec = pltpu.PrefetchScalarGridSpec(
        num_scalar_prefetch=1,
        grid=(B, nt),
        in_specs=ye_specs + [
            tok(E), tok(E), tok(D),
            pl.BlockSpec((1, D), lambda b, j, c: (0, 0)),
            pl.BlockSpec((1, 2, 6, D), lambda b, j, c: (b, 0, 0, 0)),
        ],
        out_specs=tok(D),
        scratch_shapes=[pltpu.VMEM((tt, D), F32)],
    )
    return pl.pallas_call(
        functools.partial(_moe_scatter_kernel, nt=nt, tt=tt, cs=cs, nk=nk, mod_row=mod_row, n_e=E),
        grid_spec=grid_spec,
        out_shape=jax.ShapeDtypeStruct((B, R, D), F32),
        input_output_aliases={2 * E + 3: 0},
        compiler_params=_cparams(("parallel", "parallel")),
    )(cnt, *([ye] * (2 * E)), rsel_te, aff_te, xa, norm_g.reshape(1, D), mods)


def _moe(xa, h, aff, wg, wu, wd, norm_g, mods, *, row0, n_tok, mod_row):
    B, R, D = xa.shape
    E = aff.shape[2]
    cap = EC_CAPACITY_FACTOR * n_tok // E
    if n_tok <= 256:
        tt, cs = n_tok, cap
    else:
        tt = cs = min(256, cap)
    aff_et = jnp.swapaxes(aff[:, row0:row0 + n_tok], 1, 2)
    rank, rsel = _select(aff_et, cap)
    cnt = jnp.concatenate([rank[:, :, ::tt], jnp.full((B, E, 1), cap, I32)], axis=2).reshape(-1)
    ye = _moe_ffn(cnt, h, rsel, wg, wu, wd, row0=row0, n_tok=n_tok, cap=cap, tt=tt, cs=cs)
    rsel_te = jnp.pad(jnp.swapaxes(rsel, 1, 2), ((0, 0), (row0, R - row0 - n_tok), (0, 0)), constant_values=-1)
    return _moe_scatter(cnt, ye, rsel_te, aff, xa, norm_g, mods,
                        row0=row0, n_tok=n_tok, cap=cap, tt=tt, cs=cs, mod_row=mod_row)


def _rope_tables(n_ctx, n_lat):
    lane = jnp.arange(LANES) % DA_QK_DIM
    axis, freq = lane // (2 * ROPE_FREQS), lane % ROPE_FREQS
    sign = jnp.where((lane % (2 * ROPE_FREQS)) < ROPE_FREQS, -1.0, 1.0)
    t = jnp.arange(n_lat)
    pos = jnp.where(axis[None, :] == 0, (t // GRID_W)[:, None], (t % GRID_W)[:, None]).astype(F32)
    ang = pos * (ROPE_BASE ** (-freq.astype(F32) / ROPE_FREQS))[None, :]
    cos = jnp.concatenate([jnp.ones((n_ctx, LANES), F32), jnp.cos(ang)], axis=0)
    sin = jnp.concatenate([jnp.zeros((n_ctx, LANES), F32), jnp.sin(ang) * sign], axis=0)
    return cos, sin


def _layout_w_in(w):
    parts, off = [], 0
    for n in IN_SIZES:
        parts.append(w[:, off:off + n])
        off += n
    wq, wk, wv, g_qkv, g_gate, g_a, g_b, r_rkv, r_xw, r_xa, r_xg = parts
    partner = jnp.arange(512) ^ ROPE_FREQS
    misc = jnp.concatenate([g_a, g_b, r_xw, r_xa, r_xg], axis=1)
    misc = jnp.pad(misc, ((0, 0), (0, WIDTH - misc.shape[1])))
    return jnp.concatenate([wq, wq[:, partner], wk, wk[:, partner], wv, g_qkv, r_rkv, g_gate, misc],
                           axis=1).astype(BF16)


def kernel(x, c, ctx, c_ctx, w_mod, b_mod, norm_pre, norm_post, w_in, w_out, da_lambda, da_norm, gdn_conv,
           gdn_a_log, gdn_dt_bias, gdn_norm, rwkv_w0, rwkv_w_up, rwkv_a0, rwkv_a_up, rwkv_g_up, rwkv_k_k,
           rwkv_k_a, rwkv_r_k, rwkv_gn_w, rwkv_gn_b, moe_router, moe_w_gate, moe_w_up, moe_w_down):
    B, T, D = x.shape
    n_ctx = ctx.shape[1]
    depth = w_mod.shape[0]
    cos, sin = _rope_tables(n_ctx, T)
    cond = jnp.concatenate([jax.nn.silu(c_ctx.astype(F32))[None], jax.nn.silu(c.astype(F32))], axis=0)
    xa = jnp.concatenate([ctx.astype(x.dtype), x], axis=1)

    for i in range(depth):
        last = i == depth - 1
        m = jnp.dot(cond, w_mod[i], precision=HI) + b_mod[i]
        mods = jnp.stack([jnp.broadcast_to(m[:1], (B, 6 * D)), m[1:]], axis=1).reshape(B, 2, 6, D)

        qa, ka, va, rest = _inproj(xa, norm_pre[i, 0], mods, _layout_w_in(w_in[i]), cos, sin, n_ctx)

        lam_init = 0.8 - 0.6 * math.exp(-0.3 * i)
        lv = da_lambda[i].astype(F32)
        lam = jnp.exp(jnp.sum(lv[0] * lv[1])) - jnp.exp(jnp.sum(lv[2] * lv[3])) + lam_init
        a_out = _attention(qa, ka, va, lam, da_norm[i], n_ctx=n_ctx, out_scale=1.0 - lam_init)

        g_f, g_b = _scan(_gdn_prep(rest, gdn_conv[i], gdn_a_log[i], gdn_dt_bias[i], n_ctx), _gdn_step, n_ctx)
        rw = _rwkv_prep(rest, rwkv_w_up[i], rwkv_a_up[i], rwkv_g_up[i], rwkv_w0[i], rwkv_a0[i],
                        rwkv_k_k[i], rwkv_k_a[i], rwkv_r_k[i])
        r_f, r_b = _scan(rw, _rwkv_step, n_ctx)

        xa = _outproj(a_out, g_f, g_b, r_f, r_b, rest, rw, xa, w_out[i].astype(BF16), gdn_norm[i],
                      rwkv_gn_w[i], rwkv_gn_b[i], norm_post[i, 0], mods, n_ctx)

        h, aff = _router(xa, norm_pre[i, 1], mods, moe_router[i], n_ctx)
        wg, wu, wd = moe_w_gate[i].astype(BF16), moe_w_up[i].astype(BF16), moe_w_down[i].astype(BF16)
        xa = _moe(xa, h, aff, wg, wu, wd, norm_post[i, 1], mods, row0=n_ctx, n_tok=T, mod_row=1)
        if not last:
            xa = _moe(xa, h, aff, wg, wu, wd, norm_post[i, 1], mods, row0=0, n_tok=n_ctx, mod_row=0)
    return xa[:, n_ctx:]
```

```python
import functools
import math

import jax
import jax.numpy as jnp
from jax import lax
from jax.experimental import pallas as pl
from jax.experimental.pallas import tpu as pltpu

F32 = jnp.float32
BF16 = jnp.bfloat16
I32 = jnp.int32
HI = lax.Precision.HIGHEST

GRID_W = 64
NORM_EPS = 1e-6
DA_HEADS = 4
DA_QK_DIM = 64
DA_V_DIM = 128
DA_WIDTH = DA_HEADS * DA_V_DIM
ROPE_BASE = 10000.0
ROPE_FREQS = DA_QK_DIM // 4
HEADS = 4
HEAD_DIM = 64
WIDTH = HEADS * HEAD_DIM
GDN_CONV = 5
CHUNK = 64
RWKV_W_LORA = 32
RWKV_A_LORA = 32
RWKV_G_LORA = 64
RWKV_DECAY_SCALE = 0.6065306597126334
RWKV_GN_EPS = 64e-5
N_EXPERTS = 16
EC_CAPACITY_FACTOR = 2
N_DIRS = 2
IN_SIZES = (512, 512, 512, 768, 256, 8, 8, 768, 32, 32, 64)
LANES = 128
SUBLANES = 8
VMEM_LIMIT = 56 * 1024 * 1024

W_ROPE = 4 * 512
W_V = 512
W_REST = 2048
M_A, M_B, M_XW, M_XA, M_XG = 0, 8, 16, 48, 80
GDN_PACK = 7
RWKV_PACK = 11


def _pick(n, cands):
    for c in cands:
        if n % c == 0:
            return c
    raise ValueError(f"no tile for {n} in {cands}")


def _cparams(sem):
    return pltpu.CompilerParams(dimension_semantics=sem, vmem_limit_bytes=VMEM_LIMIT)


def _rms(x, g):
    return x * lax.rsqrt(jnp.mean(x * x, axis=-1, keepdims=True) + NORM_EPS) * g


def _row_mod(mods_ref, idx, row0, n_rows, n_ctx):
    rows = row0 + lax.broadcasted_iota(I32, (n_rows, 1), 0)
    return jnp.where(rows < n_ctx, mods_ref[0, 0, idx:idx + 1, :], mods_ref[0, 1, idx:idx + 1, :])


def _block(i, size):
    return pl.ds(i * size, size) if isinstance(i, int) else pl.ds(pl.multiple_of(i * size, size), size)


def _split3(x):
    hi = x.astype(BF16)
    r = x - hi.astype(F32)
    mid = r.astype(BF16)
    return hi, mid, (r - mid.astype(F32)).astype(BF16)


def _mm_exact_rhs(x, sel):
    return sum(jnp.dot(t, sel, preferred_element_type=F32) for t in _split3(x))


def _mm_hi(a, b):
    ah = a.astype(BF16)
    al = (a - ah.astype(F32)).astype(BF16)
    bh = b.astype(BF16)
    bl = (b - bh.astype(F32)).astype(BF16)
    d = lambda u, v: jnp.dot(u, v, preferred_element_type=F32)
    return d(ah, bh) + d(ah, bl) + d(al, bh)


def _head_ones():
    shift = int(math.log2(HEAD_DIM))
    r = lax.broadcasted_iota(I32, (WIDTH, WIDTH), 0) >> shift
    c = lax.broadcasted_iota(I32, (WIDTH, WIDTH), 1) >> shift
    return r == c


def _inproj_kernel(x_ref, g_ref, mods_ref, w_ref, cos_ref, sin_ref, q_ref, k_ref, v_ref, rest_ref, *, tm, n_ctx):
    row0 = pl.program_id(1) * tm
    h = _rms(x_ref[0], g_ref[...])
    h = h * (1.0 + _row_mod(mods_ref, 1, row0, tm, n_ctx)) + _row_mod(mods_ref, 0, row0, tm, n_ctx)
    hb = h.astype(BF16)
    cos = jnp.concatenate([cos_ref[...]] * 4, axis=1)
    sin = jnp.concatenate([sin_ref[...]] * 4, axis=1)
    proj = lambda a, b: jnp.dot(hb, w_ref[:, a:b], preferred_element_type=F32)
    q_ref[0] = ((proj(0, 512) * cos + proj(512, 1024) * sin) * DA_QK_DIM ** -0.5).astype(BF16)
    k_ref[0] = (proj(1024, 1536) * cos + proj(1536, 2048) * sin).astype(BF16)
    v_ref[0] = proj(W_ROPE, W_ROPE + W_V).astype(BF16)
    rest_ref[0] = proj(W_ROPE + W_V, W_ROPE + W_V + W_REST)


def _inproj(xa, g, mods, w, cos, sin, n_ctx):
    B, R, D = xa.shape
    tm = _pick(R, (384, 256, 128, 64))
    row = lambda n: pl.BlockSpec((1, tm, n), lambda b, i: (b, i, 0))
    return pl.pallas_call(
        functools.partial(_inproj_kernel, tm=tm, n_ctx=n_ctx),
        grid=(B, R // tm),
        in_specs=[
            row(D),
            pl.BlockSpec((1, D), lambda b, i: (0, 0)),
            pl.BlockSpec((1, 2, 6, D), lambda b, i: (b, 0, 0, 0)),
            pl.BlockSpec(w.shape, lambda b, i: (0, 0)),
            pl.BlockSpec((tm, LANES), lambda b, i: (i, 0)),
            pl.BlockSpec((tm, LANES), lambda b, i: (i, 0)),
        ],
        out_specs=[row(512), row(512), row(W_V), row(W_REST)],
        out_shape=[jax.ShapeDtypeStruct((B, R, 512), BF16), jax.ShapeDtypeStruct((B, R, 512), BF16),
                   jax.ShapeDtypeStruct((B, R, W_V), BF16), jax.ShapeDtypeStruct((B, R, W_REST), F32)],
        compiler_params=_cparams(("parallel", "parallel")),
    )(xa, g.reshape(1, D), mods, w, cos, sin)


def _attn_kernel(lam_ref, q_ref, k_ref, v_ref, g_ref, o_ref, *, hb, tq, tk_ctx, tk_all, n_ctx, n_rows, out_scale):
    heads = lambda x: jnp.stack([x[:, h * 128:(h + 1) * 128] for h in range(hb)], axis=0)
    q = heads(q_ref[0])
    lane = lax.broadcasted_iota(I32, q.shape, 2)
    zero = jnp.zeros_like(q)
    qq = jnp.concatenate([jnp.where(lane < DA_QK_DIM, q, zero),
                          jnp.where(lane >= DA_QK_DIM, q, zero)], axis=1)

    def attend(tk, n_kv):
        def body(c, carry):
            m, acc = carry
            k = heads(k_ref[0, _block(c, tk), :])
            v = heads(v_ref[0, _block(c, tk), :])
            v1 = jnp.concatenate([v, jnp.ones_like(v)], axis=-1)
            s = jnp.einsum('hqd,hkd->hqk', qq, k, preferred_element_type=F32)
            m_new = jnp.maximum(m, jnp.max(s, axis=-1, keepdims=True))
            p = jnp.exp((s - m_new).astype(BF16))
            acc = jnp.exp(m - m_new) * acc + jnp.einsum('hqk,hkd->hqd', p, v1, preferred_element_type=F32)
            return m_new, acc

        m0 = jnp.full((hb, 2 * tq, 1), -jnp.inf, F32)
        a0 = jnp.zeros((hb, 2 * tq, 2 * DA_V_DIM), F32)
        _, acc = lax.fori_loop(0, n_kv, body, (m0, a0))
        o = acc[:, :, :DA_V_DIM] / acc[:, :, DA_V_DIM:DA_V_DIM + 1]
        o = o[:, :tq] - lam_ref[0] * o[:, tq:]
        o = _rms(o, g_ref[...]) * out_scale
        for h in range(hb):
            o_ref[0, :, h * 128:(h + 1) * 128] = o[h]

    is_ctx = pl.program_id(2) < n_ctx // tq

    @pl.when(is_ctx)
    def _():
        attend(tk_ctx, n_ctx // tk_ctx)

    @pl.when(jnp.logical_not(is_ctx))
    def _():
        attend(tk_all, n_rows // tk_all)


def _attention(q, k, v, lam, g, *, n_ctx, out_scale):
    B, R, _ = q.shape
    hb = 4
    tq = _pick(math.gcd(n_ctx, R), (256, 128))
    tk_ctx = _pick(n_ctx, (768, 512, 256, 128))
    tk_all = _pick(R, (768, 512, 256, 128))
    return pl.pallas_call(
        functools.partial(_attn_kernel, hb=hb, tq=tq, tk_ctx=tk_ctx, tk_all=tk_all, n_ctx=n_ctx, n_rows=R,
                          out_scale=out_scale),
        grid=(B, DA_HEADS // hb, R // tq),
        in_specs=[
            pl.BlockSpec(memory_space=pltpu.SMEM),
            pl.BlockSpec((1, tq, hb * 128), lambda b, h, i: (b, i, h)),
            pl.BlockSpec((1, R, hb * 128), lambda b, h, i: (b, 0, h)),
            pl.BlockSpec((1, R, hb * 128), lambda b, h, i: (b, 0, h)),
            pl.BlockSpec((1, DA_V_DIM), lambda b, h, i: (0, 0)),
        ],
        out_specs=pl.BlockSpec((1, tq, hb * 128), lambda b, h, i: (b, i, h)),
        out_shape=jax.ShapeDtypeStruct((B, R, DA_WIDTH), F32),
        compiler_params=_cparams(("parallel", "parallel", "parallel")),
    )(lam.reshape(1).astype(F32), q, k, v, g.reshape(1, DA_V_DIM))


def _halo_shift(cur, prev8, next8, s, first, last):
    n = cur.shape[0]
    if s == 0:
        return cur
    row = lax.broadcasted_iota(I32, (n, 1), 0)
    rolled = pltpu.roll(cur, (-s) % n, 0)
    if s < 0:
        halo = jnp.where(first, 0.0, pltpu.roll(prev8, -s, 0))
        return jnp.where(row < -s, jnp.concatenate([halo] * (n // SUBLANES), axis=0), rolled)
    halo = jnp.where(last, 0.0, pltpu.roll(next8, SUBLANES - s, 0))
    return jnp.where(row >= n - s, jnp.concatenate([halo] * (n // SUBLANES), axis=0), rolled)


def _gdn_prep_kernel(prev_ref, cur_ref, next_ref, misc_ref, cw_ref, alog_ref, dtb_ref, o_ref, *, tp, n_ctx, n_rows):
    row0 = pl.program_id(1) * tp
    first = (row0 == 0) | (row0 == n_ctx)
    last = (row0 + tp == n_ctx) | (row0 + tp == n_rows)
    cur, prev8, next8 = cur_ref[0], prev_ref[0], next_ref[0]
    pad = GDN_CONV // 2
    y = sum(_halo_shift(cur, prev8, next8, j - pad, first, last) * cw_ref[j:j + 1, :] for j in range(GDN_CONV))
    y = y * jax.nn.sigmoid(y)
    q, k, v = y[:, :WIDTH], y[:, WIDTH:2 * WIDTH], y[:, 2 * WIDTH:]
    same_head = jnp.where(_head_ones(), 1.0, 0.0).astype(BF16)
    o_ref[0, :, 0:WIDTH] = q * lax.rsqrt(_mm_exact_rhs(q * q, same_head) + 1e-12) * HEAD_DIM ** -0.5
    o_ref[0, :, WIDTH:2 * WIDTH] = k * lax.rsqrt(_mm_exact_rhs(k * k, same_head) + 1e-12)
    o_ref[0, :, 2 * WIDTH:3 * WIDTH] = v
    m = misc_ref[0]
    x = m + dtb_ref[...]
    softplus = jnp.maximum(x, 0.0) + jnp.log1p(jnp.exp(-jnp.abs(x)))
    g_all = -jnp.exp(alog_ref[...]) * softplus
    b_all = jax.nn.sigmoid(m)
    src = lax.broadcasted_iota(I32, (WIDTH, WIDTH), 0)
    head = lax.broadcasted_iota(I32, (WIDTH, WIDTH), 1) >> int(math.log2(HEAD_DIM))
    for d in range(N_DIRS):
        pick_g = jnp.where(src == M_A + d * HEADS + head, 1.0, 0.0).astype(BF16)
        pick_b = jnp.where(src == M_B + d * HEADS + head, 1.0, 0.0).astype(BF16)
        base = (3 + 2 * d) * WIDTH
        o_ref[0, :, base:base + WIDTH] = _mm_exact_rhs(g_all, pick_g)
        o_ref[0, :, base + WIDTH:base + 2 * WIDTH] = _mm_exact_rhs(b_all, pick_b)


def _halo_specs(tp, width, col_block, n_rows):
    r8 = tp // SUBLANES
    n8 = n_rows // SUBLANES
    return [
        pl.BlockSpec((1, SUBLANES, width), lambda b, i: (b, jnp.maximum(i * r8 - 1, 0), col_block)),
        pl.BlockSpec((1, tp, width), lambda b, i: (b, i, col_block)),
        pl.BlockSpec((1, SUBLANES, width), lambda b, i: (b, jnp.minimum((i + 1) * r8, n8 - 1), col_block)),
    ]


def _gdn_prep(rest, conv_w, a_log, dt_bias, n_ctx):
    B, R, _ = rest.shape
    tp = _pick(math.gcd(R, n_ctx), (256, 128, 64))
    alog = jnp.zeros((1, WIDTH), F32).at[0, M_A:M_A + N_DIRS * HEADS].set(a_log.reshape(-1))
    dtb = jnp.zeros((1, WIDTH), F32).at[0, M_A:M_A + N_DIRS * HEADS].set(dt_bias.reshape(-1))
    vec = lambda n: pl.BlockSpec((1, n), lambda b, i: (0, 0))
    return pl.pallas_call(
        functools.partial(_gdn_prep_kernel, tp=tp, n_ctx=n_ctx, n_rows=R),
        grid=(B, R // tp),
        in_specs=_halo_specs(tp, 3 * WIDTH, 0, R) + [
            pl.BlockSpec((1, tp, WIDTH), lambda b, i: (b, i, 7)),
            pl.BlockSpec((GDN_CONV, 3 * WIDTH), lambda b, i: (0, 0)),
            vec(WIDTH), vec(WIDTH)],
        out_specs=pl.BlockSpec((1, tp, GDN_PACK * WIDTH), lambda b, i: (b, i, 0)),
        out_shape=jax.ShapeDtypeStruct((B, R, GDN_PACK * WIDTH), F32),
        compiler_params=_cparams(("parallel", "parallel")),
    )(rest, rest, rest, rest, conv_w, alog, dtb)


def _rwkv_prep_kernel(rkv_ref, misc_ref, wup_ref, vec_ref, o_ref):
    rkv, m = rkv_ref[0], misc_ref[0]
    r, k, v = rkv[:, :WIDTH], rkv[:, WIDTH:2 * WIDTH], rkv[:, 2 * WIDTH:]
    k_k, k_a, r_k = vec_ref[0:1, :], vec_ref[1:2, :], vec_ref[2:3, :]
    same_head = jnp.where(_head_ones(), 1.0, 0.0).astype(BF16)
    kkr = k * k_k
    kk = kkr * lax.rsqrt(_mm_exact_rhs(kkr * kkr, same_head) + 1e-12)
    tanh_m, sig_m = jnp.tanh(m), jax.nn.sigmoid(m)
    o_ref[0, :, 0:WIDTH] = r
    o_ref[0, :, WIDTH:2 * WIDTH] = v
    o_ref[0, :, 2 * WIDTH:3 * WIDTH] = kk
    kd_sum = jnp.zeros_like(k)
    for d in range(N_DIRS):
        w0, a0 = vec_ref[3 + 2 * d:4 + 2 * d, :], vec_ref[4 + 2 * d:5 + 2 * d, :]
        logw = -RWKV_DECAY_SCALE * jax.nn.sigmoid(w0 + _mm_hi(tanh_m, wup_ref[2 * d]))
        a_d = jax.nn.sigmoid(a0 + _mm_hi(m, wup_ref[2 * d + 1]))
        kd = k * (1.0 + (a_d - 1.0) * k_a)
        kd_sum = kd_sum + kd
        base = (3 + 3 * d) * WIDTH
        o_ref[0, :, base:base + WIDTH] = kd
        o_ref[0, :, base + WIDTH:base + 2 * WIDTH] = kk * a_d
        o_ref[0, :, base + 2 * WIDTH:base + 3 * WIDTH] = logw
    o_ref[0, :, 9 * WIDTH:10 * WIDTH] = _mm_hi(sig_m, wup_ref[2 * N_DIRS])
    o_ref[0, :, 10 * WIDTH:11 * WIDTH] = _mm_exact_rhs(r * kd_sum * r_k, same_head) * v


def _rwkv_prep(rest, w_up, a_up, g_up, w0, a0, k_k, k_a, r_k):
    B, R, _ = rest.shape
    tp = _pick(R, (256, 128, 64))
    embed = lambda w, lane0: jnp.zeros((WIDTH, WIDTH), F32).at[lane0:lane0 + w.shape[0]].set(w)
    wup = jnp.stack([embed(w_up[0], M_XW), embed(a_up[0], M_XA), embed(w_up[1], M_XW), embed(a_up[1], M_XA),
                     embed(g_up, M_XG)])
    vecs = jnp.stack([k_k, k_a, r_k.reshape(-1), w0[0], a0[0], w0[1], a0[1], jnp.zeros_like(k_k)])
    return pl.pallas_call(
        _rwkv_prep_kernel,
        grid=(B, R // tp),
        in_specs=[pl.BlockSpec((1, tp, 3 * WIDTH), lambda b, i: (b, i, 1)),
                  pl.BlockSpec((1, tp, WIDTH), lambda b, i: (b, i, 7)),
                  pl.BlockSpec(wup.shape, lambda b, i: (0, 0, 0)),
                  pl.BlockSpec(vecs.shape, lambda b, i: (0, 0))],
        out_specs=pl.BlockSpec((1, tp, RWKV_PACK * WIDTH), lambda b, i: (b, i, 0)),
        out_shape=jax.ShapeDtypeStruct((B, R, RWKV_PACK * WIDTH), F32),
        compiler_params=_cparams(("parallel", "parallel")),
    )(rest, rest, wup, vecs)


def _bmm(a, b):
    return jnp.einsum('nij,njk->nik', a.astype(BF16), b.astype(BF16), preferred_element_type=F32)


def _bmm_nt(a, b):
    return jnp.einsum('nik,njk->nij', a.astype(BF16), b.astype(BF16), preferred_element_type=F32)


def _bmm_tn(a, b):
    return jnp.einsum('nki,nkj->nij', a.astype(BF16), b.astype(BF16), preferred_element_type=F32)


def _bmm_exact_lhs(sel, x):
    return sum(jnp.einsum('nij,njk->nik', sel, t, preferred_element_type=F32) for t in _split3(x))


def _scan_consts(n_batch):
    n = N_DIRS * n_batch
    wide = (n, CHUNK, HEADS * CHUNK)
    sq = (n, CHUNK, CHUNK)
    it = lambda shp, ax: lax.broadcasted_iota(I32, shp, ax)
    sign = lambda shp: jnp.where(it(shp, 0) < n_batch, 1, -1)
    delta = (it(wide, 1) - (it(wide, 2) & (CHUNK - 1))) * sign(wide)
    dsq = (it(sq, 1) - it(sq, 2)) * sign(sq)
    return dict(incl=delta >= 0, strict=delta > 0, eye=delta == 0,
                tri=jnp.where(dsq >= 0, 1.0, 0.0).astype(BF16), bd=_head_ones()[None])


def _bd(x, c):
    return jnp.where(c['bd'], jnp.concatenate([x] * HEADS, axis=1), 0.0)


def _unit_tri_inverse(a, c):
    n = -a
    p = jnp.where(c['eye'], 1.0, 0.0) + n
    nb = _bd(n, c)
    for _ in range(int(math.log2(CHUNK)) - 1):
        n = _bmm(n, nb)
        nb = _bd(n, c)
        p = p + _bmm(p, nb)
    return p


def _chunk_rev(s, n_ctx_chunks, n_chunks):
    return jnp.where(s < n_ctx_chunks, n_ctx_chunks - 1 - s, n_chunks - 1 - (s - n_ctx_chunks))


def _gdn_step(col, c, s):
    q, k, v, gw, bw = col(0, 0), col(1, 1), col(2, 2), col(3, 5), col(4, 6)
    gc = _bmm_exact_lhs(c['tri'], gw)
    ones = jnp.ones(c['tri'].shape, BF16)
    gr = _bmm_exact_lhs(ones, jnp.where(c['eye'], gc, 0.0))
    decay = jnp.exp(jnp.where(c['incl'], gc - gr, -jnp.inf))
    kb = k * bw
    km = _bd(k, c)
    a_mat = jnp.where(c['strict'], _bmm_nt(kb, km) * decay, 0.0)
    qk = _bmm_nt(q, km) * decay
    t_inv = _unit_tri_inverse(a_mat, c)
    uw = _bmm(t_inv, jnp.concatenate([_bd(v * bw, c), _bd(kb * jnp.exp(gc), c)], axis=2))
    u, w = uw[:, :, :WIDTH], uw[:, :, WIDTH:]
    g_end = jnp.min(gc, axis=1, keepdims=True)
    v_new = u - _bmm(w, s)
    o = _bmm(q * jnp.exp(gc), s) + _bmm(qk, _bd(v_new, c))
    return o, s * jnp.exp(g_end) + jnp.where(c['bd'], _bmm_tn(k * jnp.exp(g_end - gc), v_new), 0.0)


def _rwkv_step(col, c, st):
    r, v, kk, kd, b, lw = col(0, 0), col(1, 1), col(2, 2), col(3, 6), col(4, 7), col(5, 8)
    n4 = HEADS * CHUNK
    cum = _bmm_exact_lhs(c['tri'], lw)
    inv = jnp.exp(-cum)
    c_s = kk * jnp.exp(cum - lw)
    r_s = r * jnp.exp(cum)
    big = _bmm_nt(jnp.concatenate([c_s, r_s], axis=1),
                  jnp.concatenate([_bd(b * inv, c), _bd(kd * inv, c)], axis=1))
    l_cb = jnp.where(c['strict'], big[:, :CHUNK, :n4], 0.0)
    l_ck = jnp.where(c['strict'], big[:, :CHUNK, n4:], 0.0)
    a_rb = jnp.where(c['incl'], big[:, CHUNK:, :n4], 0.0)
    a_rk = jnp.where(c['incl'], big[:, CHUNK:, n4:], 0.0)
    t_inv = _unit_tri_inverse(l_cb, c)
    lv = _bmm(jnp.concatenate([l_ck, a_rk], axis=1), _bd(v, c))
    uc = _bmm(t_inv, jnp.concatenate([_bd(lv[:, :CHUNK], c), _bd(c_s, c)], axis=2))
    u1, cw = uc[:, :, :WIDTH], uc[:, :, WIDTH:]
    end = jnp.min(cum, axis=1, keepdims=True)
    su = _bmm_nt(jnp.concatenate([cw, r_s], axis=1), st)
    u = u1 + su[:, :CHUNK]
    y = lv[:, CHUNK:] + su[:, CHUNK:] - _bmm(a_rb, _bd(u, c))
    e_end = jnp.exp(end - cum)
    upd = _bmm_tn(jnp.concatenate([v, u], axis=1), jnp.concatenate([kd * e_end, -(b * e_end)], axis=1))
    return y, st * jnp.exp(end) + jnp.where(c['bd'], upd, 0.0)


def _scan_kernel(f_ref, b_ref, of_ref, ob_ref, s_ref, *, step):
    @pl.when(pl.program_id(0) == 0)
    def _():
        s_ref[...] = jnp.zeros_like(s_ref)

    n_batch = f_ref.shape[0]
    grp = lambda ref, j: ref[:, :, j * WIDTH:(j + 1) * WIDTH]
    col = lambda jf, jb: jnp.concatenate([grp(f_ref, jf), grp(b_ref, jb)], axis=0)
    o, s_new = step(col, _scan_consts(n_batch), s_ref[...])
    of_ref[...] = o[:n_batch]
    ob_ref[...] = o[n_batch:]
    s_ref[...] = s_new


def _scan(packed, step, n_ctx):
    B, R, P = packed.shape
    nc, ncc = R // CHUNK, n_ctx // CHUNK
    rev = functools.partial(_chunk_rev, n_ctx_chunks=ncc, n_chunks=nc)
    out = jax.ShapeDtypeStruct((B, R, WIDTH), F32)
    return pl.pallas_call(
        functools.partial(_scan_kernel, step=step),
        grid=(nc,),
        in_specs=[pl.BlockSpec((B, CHUNK, P), lambda s: (0, s, 0)),
                  pl.BlockSpec((B, CHUNK, P), lambda s: (0, rev(s), 0))],
        out_specs=[pl.BlockSpec((B, CHUNK, WIDTH), lambda s: (0, s, 0)),
                   pl.BlockSpec((B, CHUNK, WIDTH), lambda s: (0, rev(s), 0))],
        out_shape=[out, out],
        scratch_shapes=[pltpu.VMEM((B * N_DIRS, WIDTH, WIDTH), F32)],
        compiler_params=_cparams(("arbitrary",)),
    )(packed, packed)


def _outproj_kernel(a_ref, gf_ref, gb_ref, gg_ref, rf_ref, rb_ref, rg_ref, rbn_ref, x_ref, w_ref, vec_ref, ng_ref,
                    mods_ref, o_ref, *, tm, n_ctx):
    row0 = pl.program_id(1) * tm
    same_head = jnp.where(_head_ones(), 1.0, 0.0).astype(BF16)
    hmean = lambda t: _mm_exact_rhs(t, same_head) * (1.0 / HEAD_DIM)
    og = gf_ref[0] + gb_ref[0]
    gate = gg_ref[0]
    y_g = og * lax.rsqrt(hmean(og * og) + NORM_EPS) * vec_ref[0:1, :] * (gate * jax.nn.sigmoid(gate))
    yr = rf_ref[0] + rb_ref[0]
    dev = yr - hmean(yr)
    yn = dev * lax.rsqrt(hmean(dev * dev) + RWKV_GN_EPS) * vec_ref[1:2, :] + vec_ref[2:3, :]
    y_r = (yn + rbn_ref[0]) * rg_ref[0]
    mix = jnp.dot(a_ref[0].astype(BF16), w_ref[0:DA_WIDTH, :], preferred_element_type=F32)
    mix += jnp.dot(y_g.astype(BF16), w_ref[DA_WIDTH:DA_WIDTH + WIDTH, :], preferred_element_type=F32)
    mix += jnp.dot(y_r.astype(BF16), w_ref[DA_WIDTH + WIDTH:, :], preferred_element_type=F32)
    o_ref[0] = x_ref[0] + _row_mod(mods_ref, 2, row0, tm, n_ctx) * _rms(mix, ng_ref[...])


def _outproj(a, g_f, g_b, r_f, r_b, rest, rw, xa, w, gdn_norm, gn_w, gn_b, norm_g, mods, n_ctx):
    B, R, D = xa.shape
    tm = _pick(R, (384, 256, 128, 64))
    vecs = jnp.stack([jnp.tile(gdn_norm, HEADS), gn_w, gn_b, jnp.zeros_like(gn_w)] + [jnp.zeros_like(gn_w)] * 4)
    row = lambda n, cb=0: pl.BlockSpec((1, tm, n), lambda b, i: (b, i, cb))
    return pl.pallas_call(
        functools.partial(_outproj_kernel, tm=tm, n_ctx=n_ctx),
        grid=(B, R // tm),
        in_specs=[row(DA_WIDTH), row(WIDTH), row(WIDTH), row(WIDTH, 6), row(WIDTH), row(WIDTH),
                  row(WIDTH, 9), row(WIDTH, 10), row(D),
                  pl.BlockSpec(w.shape, lambda b, i: (0, 0)),
                  pl.BlockSpec(vecs.shape, lambda b, i: (0, 0)),
                  pl.BlockSpec((1, D), lambda b, i: (0, 0)),
                  pl.BlockSpec((1, 2, 6, D), lambda b, i: (b, 0, 0, 0))],
        out_specs=row(D),
        out_shape=jax.ShapeDtypeStruct((B, R, D), F32),
        compiler_params=_cparams(("parallel", "parallel")),
    )(a, g_f, g_b, rest, r_f, r_b, rw, rw, xa, w, vecs, norm_g.reshape(1, D), mods)


def _router_kernel(x_ref, g_ref, mods_ref, wr_ref, h_ref, aff_ref, *, tm, n_ctx):
    row0 = pl.program_id(1) * tm
    h = _rms(x_ref[0], g_ref[...])
    h = h * (1.0 + _row_mod(mods_ref, 4, row0, tm, n_ctx)) + _row_mod(mods_ref, 3, row0, tm, n_ctx)
    h_ref[0] = h.astype(BF16)
    logits = jnp.dot(h, wr_ref[...], precision=HI, preferred_element_type=F32)
    e = jnp.exp(logits - jnp.max(logits, axis=-1, keepdims=True))
    aff_ref[0] = e / jnp.sum(e, axis=-1, keepdims=True)


def _router(xa, g, mods, wr, n_ctx):
    B, R, D = xa.shape
    E = wr.shape[1]
    tm = _pick(R, (768, 512, 256, 128, 64))
    return pl.pallas_call(
        functools.partial(_router_kernel, tm=tm, n_ctx=n_ctx),
        grid=(B, R // tm),
        in_specs=[pl.BlockSpec((1, tm, D), lambda b, i: (b, i, 0)),
                  pl.BlockSpec((1, D), lambda b, i: (0, 0)),
                  pl.BlockSpec((1, 2, 6, D), lambda b, i: (b, 0, 0, 0)),
                  pl.BlockSpec((D, E), lambda b, i: (0, 0))],
        out_specs=[pl.BlockSpec((1, tm, D), lambda b, i: (b, i, 0)),
                   pl.BlockSpec((1, tm, E), lambda b, i: (b, i, 0))],
        out_shape=[jax.ShapeDtypeStruct((B, R, D), BF16), jax.ShapeDtypeStruct((B, R, E), F32)],
        compiler_params=_cparams(("parallel", "parallel")),
    )(xa, g.reshape(1, D), mods, wr)


def _select_kernel(a_ref, rank_ref, rsel_ref, *, cap, n_rows):
    a = a_ref[0]
    rows = a.shape[0]
    ri = lax.broadcasted_iota(I32, (rows, rows), 0)
    ci = lax.broadcasted_iota(I32, (rows, rows), 1)
    shift = int(math.log2(n_rows))
    same = (ri >> shift) == (ci >> shift)
    same_e = jnp.where(same, 1.0, 0.0).astype(BF16)
    before_e = jnp.where(same & (ci < ri), 1.0, 0.0).astype(BF16)
    li = lax.broadcasted_iota(I32, (LANES, LANES), 0)
    lj = lax.broadcasted_iota(I32, (LANES, LANES), 1)
    upper = jnp.where(li <= lj, 1.0, 0.0).astype(BF16)
    ones = jnp.ones((LANES, LANES), BF16)

    def row_tot(m):
        return jnp.dot(m, ones, preferred_element_type=F32).astype(BF16)

    def count(mask):
        return jnp.dot(same_e, row_tot(jnp.where(mask, 1.0, 0.0).astype(BF16)), preferred_element_type=F32)

    def prefix(mask):
        m = jnp.where(mask, 1.0, 0.0).astype(BF16)
        incl = jnp.dot(m, upper, preferred_element_type=F32)
        off = jnp.dot(before_e, row_tot(m), preferred_element_type=F32)
        return incl - m.astype(F32) + off

    def body(it, tau):
        cand = tau | jnp.left_shift(jnp.int32(1), 30 - it)
        return jnp.where(count(a >= pltpu.bitcast(cand, F32)) >= cap, cand, tau)

    tau = lax.fori_loop(0, 31, body, jnp.zeros(a.shape, I32))
    sure = a >= pltpu.bitcast(tau + 1, F32)
    band = jnp.logical_and(a >= pltpu.bitcast(tau, F32), jnp.logical_not(sure))
    need = cap - count(sure)
    sel = jnp.logical_or(sure, jnp.logical_and(band, prefix(band) < need))
    rank = prefix(sel).astype(I32)
    rank_ref[0] = rank
    rsel_ref[0] = jnp.where(sel, rank, -1)


def _select(aff_et, cap):
    B, E, T = aff_et.shape
    n_rows = T // LANES
    assert n_rows * LANES == T and n_rows & (n_rows - 1) == 0
    shp = (B, E * n_rows, LANES)
    spec = pl.BlockSpec((1, E * n_rows, LANES), lambda b: (b, 0, 0))
    rank, rsel = pl.pallas_call(
        functools.partial(_select_kernel, cap=cap, n_rows=n_rows),
        grid=(B,),
        in_specs=[spec],
        out_specs=[spec, spec],
        out_shape=[jax.ShapeDtypeStruct(shp, I32), jax.ShapeDtypeStruct(shp, I32)],
        compiler_params=_cparams(("parallel",)),
    )(aff_et.reshape(shp))
    return rank.reshape(B, E, T), rsel.reshape(B, E, T)


def _tile_chunks(cnt_ref, base, j, cs, nk):
    lo = cnt_ref[base + j]
    hi = cnt_ref[base + j + 1]
    k0 = jnp.minimum(lo // cs, nk - 1)
    k1 = jnp.minimum(jnp.maximum(hi - 1, lo) // cs, nk - 1)
    return lo, hi, k0, k1


def _moe_ffn_kernel(cnt_ref, h_ref, rs_ref, wg_ref, wu_ref, wd_ref, ye_ref, xacc_ref, *, nt, tt, cs, nk):
    b, e, j = pl.program_id(0), pl.program_id(1), pl.program_id(2)
    n_e = pl.num_programs(1)

    @pl.when(j == 0)
    def _():
        xacc_ref[...] = jnp.zeros_like(xacc_ref)

    lo, hi, k0, k1 = _tile_chunks(cnt_ref, (b * n_e + e) * (nt + 1), j, cs, nk)
    rs = rs_ref[0, 0]
    slot = lax.broadcasted_iota(I32, (cs, tt), 0)

    def gather(kc):
        onehot = jnp.where(rs == slot + kc * cs, 1.0, 0.0).astype(BF16)
        start = pl.multiple_of(kc * cs, cs)
        xacc_ref[pl.ds(start, cs), :] += jnp.dot(onehot, h_ref[0], preferred_element_type=F32)

    @pl.when(hi > lo)
    def _():
        gather(k0)

        @pl.when(k1 > k0)
        def _():
            gather(k1)

    @pl.when(j == nt - 1)
    def _():
        xe = xacc_ref[...].astype(BF16)
        gt = jnp.dot(xe, wg_ref[0], preferred_element_type=F32)
        up = jnp.dot(xe, wu_ref[0], preferred_element_type=F32)
        hid = (gt * jax.nn.sigmoid(gt) * up).astype(BF16)
        ye_ref[0, 0] = jnp.dot(hid, wd_ref[0], preferred_element_type=F32).astype(BF16)


def _moe_ffn(cnt, h, rsel_et, wg, wu, wd, *, row0, n_tok, cap, tt, cs):
    B, R, D = h.shape
    E, _, F = wg.shape
    nt, nk = n_tok // tt, cap // cs
    assert (nt == 1 and nk == 1) or tt <= cs
    off = row0 // tt
    assert off * tt == row0
    grid_spec = pltpu.PrefetchScalarGridSpec(
        num_scalar_prefetch=1,
        grid=(B, E, nt),
        in_specs=[
            pl.BlockSpec((1, tt, D), lambda b, e, j, c: (b, j + off, 0)),
            pl.BlockSpec((1, 1, 1, tt), lambda b, e, j, c: (b, e, 0, j)),
            pl.BlockSpec((1, D, F), lambda b, e, j, c: (e, 0, 0)),
            pl.BlockSpec((1, D, F), lambda b, e, j, c: (e, 0, 0)),
            pl.BlockSpec((1, F, D), lambda b, e, j, c: (e, 0, 0)),
        ],
        out_specs=pl.BlockSpec((1, 1, cap, D), lambda b, e, j, c: (b, e, 0, 0)),
        scratch_shapes=[pltpu.VMEM((cap, D), F32)],
    )
    return pl.pallas_call(
        functools.partial(_moe_ffn_kernel, nt=nt, tt=tt, cs=cs, nk=nk),
        grid_spec=grid_spec,
        out_shape=jax.ShapeDtypeStruct((B, E, cap, D), BF16),
        compiler_params=_cparams(("parallel", "parallel", "arbitrary")),
    )(cnt, h, rsel_et.reshape(B, E, 1, n_tok), wg, wu, wd)


def _moe_scatter_kernel(cnt_ref, *refs, nt, tt, cs, nk, mod_row, n_e):
    ye_refs = refs[:2 * n_e]
    rs_ref, aff_ref, x_ref, ng_ref, mods_ref, o_ref, yacc_ref = refs[2 * n_e:]
    b, j = pl.program_id(0), pl.program_id(1)
    yacc_ref[...] = jnp.zeros_like(yacc_ref)
    slot0 = lax.broadcasted_iota(I32, (tt, cs), 1)

    for e in range(n_e):
        lo, hi, k0, k1 = _tile_chunks(cnt_ref, (b * n_e + e) * (nt + 1), j, cs, nk)

        def add(ye_ref, kc, e=e):
            rcol = rs_ref[0, :, e:e + 1]
            onehot = jnp.where(rcol == slot0 + kc * cs, 1.0, 0.0).astype(BF16)
            yacc_ref[...] += aff_ref[0, :, e:e + 1] * jnp.dot(onehot, ye_ref[0, 0], preferred_element_type=F32)

        @pl.when(hi > lo)
        def _():
            add(ye_refs[2 * e], k0)

        @pl.when((hi > lo) & (k1 > k0))
        def _():
            add(ye_refs[2 * e + 1], k1)

    gate = mods_ref[0, mod_row, 5:6, :]
    o_ref[0] = x_ref[0] + gate * _rms(yacc_ref[...], ng_ref[...])


def _moe_scatter(cnt, ye, rsel_te, aff_te, xa, norm_g, mods, *, row0, n_tok, cap, tt, cs, mod_row):
    B, R, D = xa.shape
    E = ye.shape[1]
    nt, nk = n_tok // tt, cap // cs
    off = row0 // tt

    def ye_map(b, j, c, *, e, second):
        _, _, k0, k1 = _tile_chunks(c, (b * E + e) * (nt + 1), j, cs, nk)
        return (b, e, k1 if second else k0, 0)

    ye_specs = [pl.BlockSpec((1, 1, cs, D), functools.partial(ye_map, e=e, second=second))
                for e in range(E) for second in (False, True)]
    tok = lambda n: pl.BlockSpec((1, tt, n), lambda b, j, c: (b, j + off, 0))
    grid_spec = pltpu.PrefetchScalarGridSpec(
        num_scalar_prefetch=1,
        grid=(B, nt),
        in_specs=ye_specs + [
            tok(E), tok(E), tok(D),
            pl.BlockSpec((1, D), lambda b, j, c: (0, 0)),
            pl.BlockSpec((1, 2, 6, D), lambda b, j, c: (b, 0, 0, 0)),
        ],
        out_specs=tok(D),
        scratch_shapes=[pltpu.VMEM((tt, D), F32)],
    )
    return pl.pallas_call(
        functools.partial(_moe_scatter_kernel, nt=nt, tt=tt, cs=cs, nk=nk, mod_row=mod_row, n_e=E),
        grid_spec=grid_spec,
        out_shape=jax.ShapeDtypeStruct((B, R, D), F32),
        input_output_aliases={2 * E + 3: 0},
        compiler_params=_cparams(("parallel", "parallel")),
    )(cnt, *([ye] * (2 * E)), rsel_te, aff_te, xa, norm_g.reshape(1, D), mods)


def _moe(xa, h, aff, wg, wu, wd, norm_g, mods, *, row0, n_tok, mod_row):
    B, R, D = xa.shape
    E = aff.shape[2]
    cap = EC_CAPACITY_FACTOR * n_tok // E
    if n_tok <= 256:
        tt, cs = n_tok, cap
    else:
        tt = cs = min(256, cap)
    aff_et = jnp.swapaxes(aff[:, row0:row0 + n_tok], 1, 2)
    rank, rsel = _select(aff_et, cap)
    cnt = jnp.concatenate([rank[:, :, ::tt], jnp.full((B, E, 1), cap, I32)], axis=2).reshape(-1)
    ye = _moe_ffn(cnt, h, rsel, wg, wu, wd, row0=row0, n_tok=n_tok, cap=cap, tt=tt, cs=cs)
    rsel_te = jnp.pad(jnp.swapaxes(rsel, 1, 2), ((0, 0), (row0, R - row0 - n_tok), (0, 0)), constant_values=-1)
    return _moe_scatter(cnt, ye, rsel_te, aff, xa, norm_g, mods,
                        row0=row0, n_tok=n_tok, cap=cap, tt=tt, cs=cs, mod_row=mod_row)


def _rope_tables(n_ctx, n_lat):
    lane = jnp.arange(LANES) % DA_QK_DIM
    axis, freq = lane // (2 * ROPE_FREQS), lane % ROPE_FREQS
    sign = jnp.where((lane % (2 * ROPE_FREQS)) < ROPE_FREQS, -1.0, 1.0)
    t = jnp.arange(n_lat)
    pos = jnp.where(axis[None, :] == 0, (t // GRID_W)[:, None], (t % GRID_W)[:, None]).astype(F32)
    ang = pos * (ROPE_BASE ** (-freq.astype(F32) / ROPE_FREQS))[None, :]
    cos = jnp.concatenate([jnp.ones((n_ctx, LANES), F32), jnp.cos(ang)], axis=0)
    sin = jnp.concatenate([jnp.zeros((n_ctx, LANES), F32), jnp.sin(ang) * sign], axis=0)
    return cos, sin


def _layout_w_in(w):
    parts, off = [], 0
    for n in IN_SIZES:
        parts.append(w[:, off:off + n])
        off += n
    wq, wk, wv, g_qkv, g_gate, g_a, g_b, r_rkv, r_xw, r_xa, r_xg = parts
    partner = jnp.arange(512) ^ ROPE_FREQS
    misc = jnp.concatenate([g_a, g_b, r_xw, r_xa, r_xg], axis=1)
    misc = jnp.pad(misc, ((0, 0), (0, WIDTH - misc.shape[1])))
    return jnp.concatenate([wq, wq[:, partner], wk, wk[:, partner], wv, g_qkv, r_rkv, g_gate, misc],
                           axis=1).astype(BF16)


def kernel(x, c, ctx, c_ctx, w_mod, b_mod, norm_pre, norm_post, w_in, w_out, da_lambda, da_norm, gdn_conv,
           gdn_a_log, gdn_dt_bias, gdn_norm, rwkv_w0, rwkv_w_up, rwkv_a0, rwkv_a_up, rwkv_g_up, rwkv_k_k,
           rwkv_k_a, rwkv_r_k, rwkv_gn_w, rwkv_gn_b, moe_router, moe_w_gate, moe_w_up, moe_w_down):
    B, T, D = x.shape
    n_ctx = ctx.shape[1]
    depth = w_mod.shape[0]
    cos, sin = _rope_tables(n_ctx, T)
    cond = jnp.concatenate([jax.nn.silu(c_ctx.astype(F32))[None], jax.nn.silu(c.astype(F32))], axis=0)
    xa = jnp.concatenate([ctx.astype(x.dtype), x], axis=1)

    for i in range(depth):
        last = i == depth - 1
        m = jnp.dot(cond, w_mod[i], precision=HI) + b_mod[i]
        mods = jnp.stack([jnp.broadcast_to(m[:1], (B, 6 * D)), m[1:]], axis=1).reshape(B, 2, 6, D)

        qa, ka, va, rest = _inproj(xa, norm_pre[i, 0], mods, _layout_w_in(w_in[i]), cos, sin, n_ctx)

        lam_init = 0.8 - 0.6 * math.exp(-0.3 * i)
        lv = da_lambda[i].astype(F32)
        lam = jnp.exp(jnp.sum(lv[0] * lv[1])) - jnp.exp(jnp.sum(lv[2] * lv[3])) + lam_init
        a_out = _attention(qa, ka, va, lam, da_norm[i], n_ctx=n_ctx, out_scale=1.0 - lam_init)

        g_f, g_b = _scan(_gdn_prep(rest, gdn_conv[i], gdn_a_log[i], gdn_dt_bias[i], n_ctx), _gdn_step, n_ctx)
        rw = _rwkv_prep(rest, rwkv_w_up[i], rwkv_a_up[i], rwkv_g_up[i], rwkv_w0[i], rwkv_a0[i],
                        rwkv_k_k[i], rwkv_k_a[i], rwkv_r_k[i])
        r_f, r_b = _scan(rw, _rwkv_step, n_ctx)

        xa = _outproj(a_out, g_f, g_b, r_f, r_b, rest, rw, xa, w_out[i].astype(BF16), gdn_norm[i],
                      rwkv_gn_w[i], rwkv_gn_b[i], norm_post[i, 0], mods, n_ctx)

        h, aff = _router(xa, norm_pre[i, 1], mods, moe_router[i], n_ctx)
        wg, wu, wd = moe_w_gate[i].astype(BF16), moe_w_up[i].astype(BF16), moe_w_down[i].astype(BF16)
        xa = _moe(xa, h, aff, wg, wu, wd, norm_post[i, 1], mods, row0=n_ctx, n_tok=T, mod_row=1)
        if not last:
            xa = _moe(xa, h, aff, wg, wu, wd, norm_post[i, 1], mods, row0=0, n_tok=n_ctx, mod_row=0)
    return xa[:, n_ctx:]
```

```python
import functools
import math

import jax
import jax.numpy as jnp
from jax import lax
from jax.experimental import pallas as pl
from jax.experimental.pallas import tpu as pltpu

F32 = jnp.float32
BF16 = jnp.bfloat16
I32 = jnp.int32
HI = lax.Precision.HIGHEST

GRID_W = 64
NORM_EPS = 1e-6
DA_HEADS = 4
DA_QK_DIM = 64
DA_V_DIM = 128
DA_WIDTH = DA_HEADS * DA_V_DIM
ROPE_BASE = 10000.0
ROPE_FREQS = DA_QK_DIM // 4
HEADS = 4
HEAD_DIM = 64
WIDTH = HEADS * HEAD_DIM
GDN_CONV = 5
CHUNK = 64
RWKV_W_LORA = 32
RWKV_A_LORA = 32
RWKV_G_LORA = 64
RWKV_DECAY_SCALE = 0.6065306597126334
RWKV_GN_EPS = 64e-5
N_EXPERTS = 16
EC_CAPACITY_FACTOR = 2
N_DIRS = 2
IN_SIZES = (512, 512, 512, 768, 256, 8, 8, 768, 32, 32, 64)
LANES = 128
SUBLANES = 8
VMEM_LIMIT = 56 * 1024 * 1024

W_ROPE = 4 * 512
W_V = 512
W_REST = 2048
M_A, M_B, M_XW, M_XA, M_XG = 0, 8, 16, 48, 80
GDN_PACK = 7
RWKV_PACK = 11


def _pick(n, cands):
    for c in cands:
        if n % c == 0:
            return c
    raise ValueError(f"no tile for {n} in {cands}")


def _cparams(sem):
    return pltpu.CompilerParams(dimension_semantics=sem, vmem_limit_bytes=VMEM_LIMIT)


def _rms(x, g):
    return x * lax.rsqrt(jnp.mean(x * x, axis=-1, keepdims=True) + NORM_EPS) * g


def _row_mod(mods_ref, idx, row0, n_rows, n_ctx):
    rows = row0 + lax.broadcasted_iota(I32, (n_rows, 1), 0)
    return jnp.where(rows < n_ctx, mods_ref[0, 0, idx:idx + 1, :], mods_ref[0, 1, idx:idx + 1, :])


def _block(i, size):
    return pl.ds(i * size, size) if isinstance(i, int) else pl.ds(pl.multiple_of(i * size, size), size)


def _split3(x):
    hi = x.astype(BF16)
    r = x - hi.astype(F32)
    mid = r.astype(BF16)
    return hi, mid, (r - mid.astype(F32)).astype(BF16)


def _mm_exact_rhs(x, sel):
    return sum(jnp.dot(t, sel, preferred_element_type=F32) for t in _split3(x))


def _mm_hi(a, b):
    ah = a.astype(BF16)
    al = (a - ah.astype(F32)).astype(BF16)
    bh = b.astype(BF16)
    bl = (b - bh.astype(F32)).astype(BF16)
    d = lambda u, v: jnp.dot(u, v, preferred_element_type=F32)
    return d(ah, bh) + d(ah, bl) + d(al, bh)


def _head_ones():
    shift = int(math.log2(HEAD_DIM))
    r = lax.broadcasted_iota(I32, (WIDTH, WIDTH), 0) >> shift
    c = lax.broadcasted_iota(I32, (WIDTH, WIDTH), 1) >> shift
    return r == c


def _inproj_kernel(x_ref, g_ref, mods_ref, w_ref, cos_ref, sin_ref, q_ref, k_ref, v_ref, rest_ref, *, tm, n_ctx):
    row0 = pl.program_id(1) * tm
    h = _rms(x_ref[0], g_ref[...])
    h = h * (1.0 + _row_mod(mods_ref, 1, row0, tm, n_ctx)) + _row_mod(mods_ref, 0, row0, tm, n_ctx)
    hb = h.astype(BF16)
    cos = jnp.concatenate([cos_ref[...]] * 4, axis=1)
    sin = jnp.concatenate([sin_ref[...]] * 4, axis=1)
    proj = lambda a, b: jnp.dot(hb, w_ref[:, a:b], preferred_element_type=F32)
    q_ref[0] = ((proj(0, 512) * cos + proj(512, 1024) * sin) * DA_QK_DIM ** -0.5).astype(BF16)
    k_ref[0] = (proj(1024, 1536) * cos + proj(1536, 2048) * sin).astype(BF16)
    v_ref[0] = proj(W_ROPE, W_ROPE + W_V).astype(BF16)
    rest_ref[0] = proj(W_ROPE + W_V, W_ROPE + W_V + W_REST)


def _inproj(xa, g, mods, w, cos, sin, n_ctx):
    B, R, D = xa.shape
    tm = _pick(R, (384, 256, 128, 64))
    row = lambda n: pl.BlockSpec((1, tm, n), lambda b, i: (b, i, 0))
    return pl.pallas_call(
        functools.partial(_inproj_kernel, tm=tm, n_ctx=n_ctx),
        grid=(B, R // tm),
        in_specs=[
            row(D),
            pl.BlockSpec((1, D), lambda b, i: (0, 0)),
            pl.BlockSpec((1, 2, 6, D), lambda b, i: (b, 0, 0, 0)),
            pl.BlockSpec(w.shape, lambda b, i: (0, 0)),
            pl.BlockSpec((tm, LANES), lambda b, i: (i, 0)),
            pl.BlockSpec((tm, LANES), lambda b, i: (i, 0)),
        ],
        out_specs=[row(512), row(512), row(W_V), row(W_REST)],
        out_shape=[jax.ShapeDtypeStruct((B, R, 512), BF16), jax.ShapeDtypeStruct((B, R, 512), BF16),
                   jax.ShapeDtypeStruct((B, R, W_V), BF16), jax.ShapeDtypeStruct((B, R, W_REST), F32)],
        compiler_params=_cparams(("parallel", "parallel")),
    )(xa, g.reshape(1, D), mods, w, cos, sin)


def _attn_kernel(lam_ref, q_ref, k_ref, v_ref, g_ref, o_ref, *, hb, tq, tk_ctx, tk_all, n_ctx, n_rows, out_scale):
    heads = lambda x: jnp.stack([x[:, h * 128:(h + 1) * 128] for h in range(hb)], axis=0)
    q = heads(q_ref[0])
    lane = lax.broadcasted_iota(I32, q.shape, 2)
    zero = jnp.zeros_like(q)
    qq = jnp.concatenate([jnp.where(lane < DA_QK_DIM, q, zero),
                          jnp.where(lane >= DA_QK_DIM, q, zero)], axis=1)

    def attend(tk, n_kv):
        def body(c, carry):
            m, acc = carry
            k = heads(k_ref[0, _block(c, tk), :])
            v = heads(v_ref[0, _block(c, tk), :])
            v1 = jnp.concatenate([v, jnp.ones_like(v)], axis=-1)
            s = jnp.einsum('hqd,hkd->hqk', qq, k, preferred_element_type=F32)
            m_new = jnp.maximum(m, jnp.max(s, axis=-1, keepdims=True))
            p = jnp.exp((s - m_new).astype(BF16))
            acc = jnp.exp(m - m_new) * acc + jnp.einsum('hqk,hkd->hqd', p, v1, preferred_element_type=F32)
            return m_new, acc

        m0 = jnp.full((hb, 2 * tq, 1), -jnp.inf, F32)
        a0 = jnp.zeros((hb, 2 * tq, 2 * DA_V_DIM), F32)
        _, acc = lax.fori_loop(0, n_kv, body, (m0, a0))
        o = acc[:, :, :DA_V_DIM] / acc[:, :, DA_V_DIM:DA_V_DIM + 1]
        o = o[:, :tq] - lam_ref[0] * o[:, tq:]
        o = _rms(o, g_ref[...]) * out_scale
        for h in range(hb):
            o_ref[0, :, h * 128:(h + 1) * 128] = o[h]

    is_ctx = pl.program_id(2) < n_ctx // tq

    @pl.when(is_ctx)
    def _():
        attend(tk_ctx, n_ctx // tk_ctx)

    @pl.when(jnp.logical_not(is_ctx))
    def _():
        attend(tk_all, n_rows // tk_all)


def _attention(q, k, v, lam, g, *, n_ctx, out_scale):
    B, R, _ = q.shape
    hb = 4
    tq = _pick(math.gcd(n_ctx, R), (256, 128))
    tk_ctx = _pick(n_ctx, (768, 512, 256, 128))
    tk_all = _pick(R, (768, 512, 256, 128))
    return pl.pallas_call(
        functools.partial(_attn_kernel, hb=hb, tq=tq, tk_ctx=tk_ctx, tk_all=tk_all, n_ctx=n_ctx, n_rows=R,
                          out_scale=out_scale),
        grid=(B, DA_HEADS // hb, R // tq),
        in_specs=[
            pl.BlockSpec(memory_space=pltpu.SMEM),
            pl.BlockSpec((1, tq, hb * 128), lambda b, h, i: (b, i, h)),
            pl.BlockSpec((1, R, hb * 128), lambda b, h, i: (b, 0, h)),
            pl.BlockSpec((1, R, hb * 128), lambda b, h, i: (b, 0, h)),
            pl.BlockSpec((1, DA_V_DIM), lambda b, h, i: (0, 0)),
        ],
        out_specs=pl.BlockSpec((1, tq, hb * 128), lambda b, h, i: (b, i, h)),
        out_shape=jax.ShapeDtypeStruct((B, R, DA_WIDTH), F32),
        compiler_params=_cparams(("parallel", "parallel", "parallel")),
    )(lam.reshape(1).astype(F32), q, k, v, g.reshape(1, DA_V_DIM))


def _halo_shift(cur, prev8, next8, s, first, last):
    n = cur.shape[0]
    if s == 0:
        return cur
    row = lax.broadcasted_iota(I32, (n, 1), 0)
    rolled = pltpu.roll(cur, (-s) % n, 0)
    if s < 0:
        halo = jnp.where(first, 0.0, pltpu.roll(prev8, -s, 0))
        return jnp.where(row < -s, jnp.concatenate([halo] * (n // SUBLANES), axis=0), rolled)
    halo = jnp.where(last, 0.0, pltpu.roll(next8, SUBLANES - s, 0))
    return jnp.where(row >= n - s, jnp.concatenate([halo] * (n // SUBLANES), axis=0), rolled)


def _gdn_prep_kernel(prev_ref, cur_ref, next_ref, misc_ref, cw_ref, alog_ref, dtb_ref, o_ref, *, tp, n_ctx, n_rows):
    row0 = pl.program_id(1) * tp
    first = (row0 == 0) | (row0 == n_ctx)
    last = (row0 + tp == n_ctx) | (row0 + tp == n_rows)
    cur, prev8, next8 = cur_ref[0], prev_ref[0], next_ref[0]
    pad = GDN_CONV // 2
    y = sum(_halo_shift(cur, prev8, next8, j - pad, first, last) * cw_ref[j:j + 1, :] for j in range(GDN_CONV))
    y = y * jax.nn.sigmoid(y)
    q, k, v = y[:, :WIDTH], y[:, WIDTH:2 * WIDTH], y[:, 2 * WIDTH:]
    same_head = jnp.where(_head_ones(), 1.0, 0.0).astype(BF16)
    o_ref[0, :, 0:WIDTH] = q * lax.rsqrt(_mm_exact_rhs(q * q, same_head) + 1e-12) * HEAD_DIM ** -0.5
    o_ref[0, :, WIDTH:2 * WIDTH] = k * lax.rsqrt(_mm_exact_rhs(k * k, same_head) + 1e-12)
    o_ref[0, :, 2 * WIDTH:3 * WIDTH] = v
    m = misc_ref[0]
    x = m + dtb_ref[...]
    softplus = jnp.maximum(x, 0.0) + jnp.log1p(jnp.exp(-jnp.abs(x)))
    g_all = -jnp.exp(alog_ref[...]) * softplus
    b_all = jax.nn.sigmoid(m)
    src = lax.broadcasted_iota(I32, (WIDTH, WIDTH), 0)
    head = lax.broadcasted_iota(I32, (WIDTH, WIDTH), 1) >> int(math.log2(HEAD_DIM))
    for d in range(N_DIRS):
        pick_g = jnp.where(src == M_A + d * HEADS + head, 1.0, 0.0).astype(BF16)
        pick_b = jnp.where(src == M_B + d * HEADS + head, 1.0, 0.0).astype(BF16)
        base = (3 + 2 * d) * WIDTH
        o_ref[0, :, base:base + WIDTH] = _mm_exact_rhs(g_all, pick_g)
        o_ref[0, :, base + WIDTH:base + 2 * WIDTH] = _mm_exact_rhs(b_all, pick_b)


def _halo_specs(tp, width, col_block, n_rows):
    r8 = tp // SUBLANES
    n8 = n_rows // SUBLANES
    return [
        pl.BlockSpec((1, SUBLANES, width), lambda b, i: (b, jnp.maximum(i * r8 - 1, 0), col_block)),
        pl.BlockSpec((1, tp, width), lambda b, i: (b, i, col_block)),
        pl.BlockSpec((1, SUBLANES, width), lambda b, i: (b, jnp.minimum((i + 1) * r8, n8 - 1), col_block)),
    ]


def _gdn_prep(rest, conv_w, a_log, dt_bias, n_ctx):
    B, R, _ = rest.shape
    tp = _pick(math.gcd(R, n_ctx), (256, 128, 64))
    alog = jnp.zeros((1, WIDTH), F32).at[0, M_A:M_A + N_DIRS * HEADS].set(a_log.reshape(-1))
    dtb = jnp.zeros((1, WIDTH), F32).at[0, M_A:M_A + N_DIRS * HEADS].set(dt_bias.reshape(-1))
    vec = lambda n: pl.BlockSpec((1, n), lambda b, i: (0, 0))
    return pl.pallas_call(
        functools.partial(_gdn_prep_kernel, tp=tp, n_ctx=n_ctx, n_rows=R),
        grid=(B, R // tp),
        in_specs=_halo_specs(tp, 3 * WIDTH, 0, R) + [
            pl.BlockSpec((1, tp, WIDTH), lambda b, i: (b, i, 7)),
            pl.BlockSpec((GDN_CONV, 3 * WIDTH), lambda b, i: (0, 0)),
            vec(WIDTH), vec(WIDTH)],
        out_specs=pl.BlockSpec((1, tp, GDN_PACK * WIDTH), lambda b, i: (b, i, 0)),
        out_shape=jax.ShapeDtypeStruct((B, R, GDN_PACK * WIDTH), F32),
        compiler_params=_cparams(("parallel", "parallel")),
    )(rest, rest, rest, rest, conv_w, alog, dtb)


def _rwkv_prep_kernel(rkv_ref, misc_ref, wup_ref, vec_ref, o_ref):
    rkv, m = rkv_ref[0], misc_ref[0]
    r, k, v = rkv[:, :WIDTH], rkv[:, WIDTH:2 * WIDTH], rkv[:, 2 * WIDTH:]
    k_k, k_a, r_k = vec_ref[0:1, :], vec_ref[1:2, :], vec_ref[2:3, :]
    same_head = jnp.where(_head_ones(), 1.0, 0.0).astype(BF16)
    kkr = k * k_k
    kk = kkr * lax.rsqrt(_mm_exact_rhs(kkr * kkr, same_head) + 1e-12)
    tanh_m, sig_m = jnp.tanh(m), jax.nn.sigmoid(m)
    o_ref[0, :, 0:WIDTH] = r
    o_ref[0, :, WIDTH:2 * WIDTH] = v
    o_ref[0, :, 2 * WIDTH:3 * WIDTH] = kk
    kd_sum = jnp.zeros_like(k)
    for d in range(N_DIRS):
        w0, a0 = vec_ref[3 + 2 * d:4 + 2 * d, :], vec_ref[4 + 2 * d:5 + 2 * d, :]
        logw = -RWKV_DECAY_SCALE * jax.nn.sigmoid(w0 + _mm_hi(tanh_m, wup_ref[2 * d]))
        a_d = jax.nn.sigmoid(a0 + _mm_hi(m, wup_ref[2 * d + 1]))
        kd = k * (1.0 + (a_d - 1.0) * k_a)
        kd_sum = kd_sum + kd
        base = (3 + 3 * d) * WIDTH
        o_ref[0, :, base:base + WIDTH] = kd
        o_ref[0, :, base + WIDTH:base + 2 * WIDTH] = kk * a_d
        o_ref[0, :, base + 2 * WIDTH:base + 3 * WIDTH] = logw
    o_ref[0, :, 9 * WIDTH:10 * WIDTH] = _mm_hi(sig_m, wup_ref[2 * N_DIRS])
    o_ref[0, :, 10 * WIDTH:11 * WIDTH] = _mm_exact_rhs(r * kd_sum * r_k, same_head) * v


def _rwkv_prep(rest, w_up, a_up, g_up, w0, a0, k_k, k_a, r_k):
    B, R, _ = rest.shape
    tp = _pick(R, (256, 128, 64))
    embed = lambda w, lane0: jnp.zeros((WIDTH, WIDTH), F32).at[lane0:lane0 + w.shape[0]].set(w)
    wup = jnp.stack([embed(w_up[0], M_XW), embed(a_up[0], M_XA), embed(w_up[1], M_XW), embed(a_up[1], M_XA),
                     embed(g_up, M_XG)])
    vecs = jnp.stack([k_k, k_a, r_k.reshape(-1), w0[0], a0[0], w0[1], a0[1], jnp.zeros_like(k_k)])
    return pl.pallas_call(
        _rwkv_prep_kernel,
        grid=(B, R // tp),
        in_specs=[pl.BlockSpec((1, tp, 3 * WIDTH), lambda b, i: (b, i, 1)),
                  pl.BlockSpec((1, tp, WIDTH), lambda b, i: (b, i, 7)),
                  pl.BlockSpec(wup.shape, lambda b, i: (0, 0, 0)),
                  pl.BlockSpec(vecs.shape, lambda b, i: (0, 0))],
        out_specs=pl.BlockSpec((1, tp, RWKV_PACK * WIDTH), lambda b, i: (b, i, 0)),
        out_shape=jax.ShapeDtypeStruct((B, R, RWKV_PACK * WIDTH), F32),
        compiler_params=_cparams(("parallel", "parallel")),
    )(rest, rest, wup, vecs)


def _bmm(a, b):
    return jnp.einsum('nij,njk->nik', a.astype(BF16), b.astype(BF16), preferred_element_type=F32)


def _bmm_nt(a, b):
    return jnp.einsum('nik,njk->nij', a.astype(BF16), b.astype(BF16), preferred_element_type=F32)


def _bmm_tn(a, b):
    return jnp.einsum('nki,nkj->nij', a.astype(BF16), b.astype(BF16), preferred_element_type=F32)


def _bmm_exact_lhs(sel, x):
    return sum(jnp.einsum('nij,njk->nik', sel, t, preferred_element_type=F32) for t in _split3(x))


def _scan_consts(n_batch, n_groups):
    n = N_DIRS * n_batch * n_groups
    wide = (n, CHUNK, HEADS * CHUNK)
    sq = (n, CHUNK, CHUNK)
    it = lambda shp, ax: lax.broadcasted_iota(I32, shp, ax)

    def sign(shp):
        idx = it(shp, 0)
        bwd = functools.reduce(jnp.logical_or, [(idx >= (2 * g + 1) * n_batch) & (idx < (2 * g + 2) * n_batch)
                                                for g in range(n_groups)])
        return jnp.where(bwd, -1, 1)

    delta = (it(wide, 1) - (it(wide, 2) & (CHUNK - 1))) * sign(wide)
    dsq = (it(sq, 1) - it(sq, 2)) * sign(sq)
    return dict(incl=delta >= 0, strict=delta > 0, eye=delta == 0,
                tri=jnp.where(dsq >= 0, 1.0, 0.0).astype(BF16), bd=_head_ones()[None])


def _bd(x, c):
    return jnp.where(c['bd'], jnp.concatenate([x] * HEADS, axis=1), 0.0)


def _unit_tri_inverse(a, c):
    n = -a
    p = jnp.where(c['eye'], 1.0, 0.0) + n
    nb = _bd(n, c)
    for _ in range(int(math.log2(CHUNK)) - 1):
        n = _bmm(n, nb)
        nb = _bd(n, c)
        p = p + _bmm(p, nb)
    return p


def _chunk_rev(s, n_ctx_chunks, n_chunks):
    return jnp.where(s < n_ctx_chunks, n_ctx_chunks - 1 - s, n_chunks - 1 - (s - n_ctx_chunks))


def _gdn_prep_chunk(col, c):
    q, k, v, gw, bw = col(0, 0), col(1, 1), col(2, 2), col(3, 5), col(4, 6)
    gc = _bmm_exact_lhs(c['tri'], gw)
    ones = jnp.ones(c['tri'].shape, BF16)
    gr = _bmm_exact_lhs(ones, jnp.where(c['eye'], gc, 0.0))
    decay = jnp.exp(jnp.where(c['incl'], gc - gr, -jnp.inf))
    kb = k * bw
    km = _bd(k, c)
    a_mat = jnp.where(c['strict'], _bmm_nt(kb, km) * decay, 0.0)
    qk = _bmm_nt(q, km) * decay
    t_inv = _unit_tri_inverse(a_mat, c)
    uw = _bmm(t_inv, jnp.concatenate([_bd(v * bw, c), _bd(kb * jnp.exp(gc), c)], axis=2))
    u, w = uw[:, :, :WIDTH], uw[:, :, WIDTH:]
    g_end = jnp.min(gc, axis=1, keepdims=True)
    return dict(u=u, w=w, q_dec=q * jnp.exp(gc), qk=qk, k_dec=k * jnp.exp(g_end - gc), keep=jnp.exp(g_end))


def _gdn_update(p, c, s):
    v_new = p['u'] - _bmm(p['w'], s)
    o = _bmm(p['q_dec'], s) + _bmm(p['qk'], _bd(v_new, c))
    return o, s * p['keep'] + jnp.where(c['bd'], _bmm_tn(p['k_dec'], v_new), 0.0)


def _rwkv_prep_chunk(col, c):
    r, v, kk, kd, b, lw = col(0, 0), col(1, 1), col(2, 2), col(3, 6), col(4, 7), col(5, 8)
    n4 = HEADS * CHUNK
    cum = _bmm_exact_lhs(c['tri'], lw)
    inv = jnp.exp(-cum)
    c_s = kk * jnp.exp(cum - lw)
    r_s = r * jnp.exp(cum)
    big = _bmm_nt(jnp.concatenate([c_s, r_s], axis=1),
                  jnp.concatenate([_bd(b * inv, c), _bd(kd * inv, c)], axis=1))
    l_cb = jnp.where(c['strict'], big[:, :CHUNK, :n4], 0.0)
    l_ck = jnp.where(c['strict'], big[:, :CHUNK, n4:], 0.0)
    a_rb = jnp.where(c['incl'], big[:, CHUNK:, :n4], 0.0)
    a_rk = jnp.where(c['incl'], big[:, CHUNK:, n4:], 0.0)
    t_inv = _unit_tri_inverse(l_cb, c)
    lv = _bmm(jnp.concatenate([l_ck, a_rk], axis=1), _bd(v, c))
    uc = _bmm(t_inv, jnp.concatenate([_bd(lv[:, :CHUNK], c), _bd(c_s, c)], axis=2))
    u1, cw = uc[:, :, :WIDTH], uc[:, :, WIDTH:]
    end = jnp.min(cum, axis=1, keepdims=True)
    e_end = jnp.exp(end - cum)
    return dict(u1=u1, cw_r=jnp.concatenate([cw, r_s], axis=1), y1=lv[:, CHUNK:], a_rb=a_rb, v=v,
                kb_end=jnp.concatenate([kd * e_end, -(b * e_end)], axis=1), keep=jnp.exp(end))


def _rwkv_update(p, c, st):
    su = _bmm_nt(p['cw_r'], st)
    u = p['u1'] + su[:, :CHUNK]
    y = p['y1'] + su[:, CHUNK:] - _bmm(p['a_rb'], _bd(u, c))
    upd = _bmm_tn(jnp.concatenate([p['v'], u], axis=1), p['kb_end'])
    return y, st * p['keep'] + jnp.where(c['bd'], upd, 0.0)


SCAN_SUB = 4


def _scan_kernel(f_ref, b_ref, of_ref, ob_ref, s_ref, *, prep, update):
    @pl.when(pl.program_id(0) == 0)
    def _():
        s_ref[...] = jnp.zeros_like(s_ref)

    n_batch = f_ref.shape[0]
    rows = lambda i: slice(i * CHUNK, (i + 1) * CHUNK)
    grp = lambda ref, i, j: ref[:, rows(i), j * WIDTH:(j + 1) * WIDTH]
    col = lambda jf, jb: jnp.concatenate(
        [x for i in range(SCAN_SUB) for x in (grp(f_ref, i, jf), grp(b_ref, SCAN_SUB - 1 - i, jb))], axis=0)
    c = _scan_consts(n_batch, SCAN_SUB)
    prepared = prep(col, c)
    n = N_DIRS * n_batch
    s = s_ref[...]
    for i in range(SCAN_SUB):
        o, s = update({k: v[i * n:(i + 1) * n] for k, v in prepared.items()}, c, s)
        of_ref[:, rows(i), :] = o[:n_batch]
        ob_ref[:, rows(SCAN_SUB - 1 - i), :] = o[n_batch:]
    s_ref[...] = s


def _scan(packed, prep, update, n_ctx):
    B, R, P = packed.shape
    rows = SCAN_SUB * CHUNK
    assert R % rows == 0 and n_ctx % rows == 0
    rev = functools.partial(_chunk_rev, n_ctx_chunks=n_ctx // rows, n_chunks=R // rows)
    out = jax.ShapeDtypeStruct((B, R, WIDTH), F32)
    return pl.pallas_call(
        functools.partial(_scan_kernel, prep=prep, update=update),
        grid=(R // rows,),
        in_specs=[pl.BlockSpec((B, rows, P), lambda s: (0, s, 0)),
                  pl.BlockSpec((B, rows, P), lambda s: (0, rev(s), 0))],
        out_specs=[pl.BlockSpec((B, rows, WIDTH), lambda s: (0, s, 0)),
                   pl.BlockSpec((B, rows, WIDTH), lambda s: (0, rev(s), 0))],
        out_shape=[out, out],
        scratch_shapes=[pltpu.VMEM((B * N_DIRS, WIDTH, WIDTH), F32)],
        compiler_params=_cparams(("arbitrary",)),
    )(packed, packed)


def _outproj_kernel(a_ref, gf_ref, gb_ref, gg_ref, rf_ref, rb_ref, rg_ref, rbn_ref, x_ref, w_ref, vec_ref, ng_ref,
                    mods_ref, o_ref, *, tm, n_ctx):
    row0 = pl.program_id(1) * tm
    same_head = jnp.where(_head_ones(), 1.0, 0.0).astype(BF16)
    hmean = lambda t: _mm_exact_rhs(t, same_head) * (1.0 / HEAD_DIM)
    og = gf_ref[0] + gb_ref[0]
    gate = gg_ref[0]
    y_g = og * lax.rsqrt(hmean(og * og) + NORM_EPS) * vec_ref[0:1, :] * (gate * jax.nn.sigmoid(gate))
    yr = rf_ref[0] + rb_ref[0]
    dev = yr - hmean(yr)
    yn = dev * lax.rsqrt(hmean(dev * dev) + RWKV_GN_EPS) * vec_ref[1:2, :] + vec_ref[2:3, :]
    y_r = (yn + rbn_ref[0]) * rg_ref[0]
    mix = jnp.dot(a_ref[0].astype(BF16), w_ref[0:DA_WIDTH, :], preferred_element_type=F32)
    mix += jnp.dot(y_g.astype(BF16), w_ref[DA_WIDTH:DA_WIDTH + WIDTH, :], preferred_element_type=F32)
    mix += jnp.dot(y_r.astype(BF16), w_ref[DA_WIDTH + WIDTH:, :], preferred_element_type=F32)
    o_ref[0] = x_ref[0] + _row_mod(mods_ref, 2, row0, tm, n_ctx) * _rms(mix, ng_ref[...])


def _outproj(a, g_f, g_b, r_f, r_b, rest, rw, xa, w, gdn_norm, gn_w, gn_b, norm_g, mods, n_ctx):
    B, R, D = xa.shape
    tm = _pick(R, (384, 256, 128, 64))
    vecs = jnp.stack([jnp.tile(gdn_norm, HEADS), gn_w, gn_b, jnp.zeros_like(gn_w)] + [jnp.zeros_like(gn_w)] * 4)
    row = lambda n, cb=0: pl.BlockSpec((1, tm, n), lambda b, i: (b, i, cb))
    return pl.pallas_call(
        functools.partial(_outproj_kernel, tm=tm, n_ctx=n_ctx),
        grid=(B, R // tm),
        in_specs=[row(DA_WIDTH), row(WIDTH), row(WIDTH), row(WIDTH, 6), row(WIDTH), row(WIDTH),
                  row(WIDTH, 9), row(WIDTH, 10), row(D),
                  pl.BlockSpec(w.shape, lambda b, i: (0, 0)),
                  pl.BlockSpec(vecs.shape, lambda b, i: (0, 0)),
                  pl.BlockSpec((1, D), lambda b, i: (0, 0)),
                  pl.BlockSpec((1, 2, 6, D), lambda b, i: (b, 0, 0, 0))],
        out_specs=row(D),
        out_shape=jax.ShapeDtypeStruct((B, R, D), F32),
        compiler_params=_cparams(("parallel", "parallel")),
    )(a, g_f, g_b, rest, r_f, r_b, rw, rw, xa, w, vecs, norm_g.reshape(1, D), mods)


def _router_kernel(x_ref, g_ref, mods_ref, wr_ref, h_ref, aff_ref, *, tm, n_ctx):
    row0 = pl.program_id(1) * tm
    h = _rms(x_ref[0], g_ref[...])
    h = h * (1.0 + _row_mod(mods_ref, 4, row0, tm, n_ctx)) + _row_mod(mods_ref, 3, row0, tm, n_ctx)
    h_ref[0] = h.astype(BF16)
    logits = jnp.dot(h, wr_ref[...], precision=HI, preferred_element_type=F32)
    e = jnp.exp(logits - jnp.max(logits, axis=-1, keepdims=True))
    aff_ref[0] = e / jnp.sum(e, axis=-1, keepdims=True)


def _router(xa, g, mods, wr, n_ctx):
    B, R, D = xa.shape
    E = wr.shape[1]
    tm = _pick(R, (768, 512, 256, 128, 64))
    return pl.pallas_call(
        functools.partial(_router_kernel, tm=tm, n_ctx=n_ctx),
        grid=(B, R // tm),
        in_specs=[pl.BlockSpec((1, tm, D), lambda b, i: (b, i, 0)),
                  pl.BlockSpec((1, D), lambda b, i: (0, 0)),
                  pl.BlockSpec((1, 2, 6, D), lambda b, i: (b, 0, 0, 0)),
                  pl.BlockSpec((D, E), lambda b, i: (0, 0))],
        out_specs=[pl.BlockSpec((1, tm, D), lambda b, i: (b, i, 0)),
                   pl.BlockSpec((1, tm, E), lambda b, i: (b, i, 0))],
        out_shape=[jax.ShapeDtypeStruct((B, R, D), BF16), jax.ShapeDtypeStruct((B, R, E), F32)],
        compiler_params=_cparams(("parallel", "parallel")),
    )(xa, g.reshape(1, D), mods, wr)


def _select_kernel(a_ref, rank_ref, rsel_ref, *, cap, n_rows):
    a = a_ref[0]
    rows = a.shape[0]
    ri = lax.broadcasted_iota(I32, (rows, rows), 0)
    ci = lax.broadcasted_iota(I32, (rows, rows), 1)
    shift = int(math.log2(n_rows))
    same = (ri >> shift) == (ci >> shift)
    same_e = jnp.where(same, 1.0, 0.0).astype(BF16)
    before_e = jnp.where(same & (ci < ri), 1.0, 0.0).astype(BF16)
    li = lax.broadcasted_iota(I32, (LANES, LANES), 0)
    lj = lax.broadcasted_iota(I32, (LANES, LANES), 1)
    upper = jnp.where(li <= lj, 1.0, 0.0).astype(BF16)
    ones = jnp.ones((LANES, LANES), BF16)

    def row_tot(m):
        return jnp.dot(m, ones, preferred_element_type=F32).astype(BF16)

    def count(mask):
        return jnp.dot(same_e, row_tot(jnp.where(mask, 1.0, 0.0).astype(BF16)), preferred_element_type=F32)

    def prefix(mask):
        m = jnp.where(mask, 1.0, 0.0).astype(BF16)
        incl = jnp.dot(m, upper, preferred_element_type=F32)
        off = jnp.dot(before_e, row_tot(m), preferred_element_type=F32)
        return incl - m.astype(F32) + off

    def body(it, tau):
        cand = tau | jnp.left_shift(jnp.int32(1), 30 - it)
        return jnp.where(count(a >= pltpu.bitcast(cand, F32)) >= cap, cand, tau)

    tau = lax.fori_loop(0, 31, body, jnp.zeros(a.shape, I32))
    sure = a >= pltpu.bitcast(tau + 1, F32)
    band = jnp.logical_and(a >= pltpu.bitcast(tau, F32), jnp.logical_not(sure))
    need = cap - count(sure)
    sel = jnp.logical_or(sure, jnp.logical_and(band, prefix(band) < need))
    rank = prefix(sel).astype(I32)
    rank_ref[0] = rank
    rsel_ref[0] = jnp.where(sel, rank, -1)


def _select(aff_et, cap):
    B, E, T = aff_et.shape
    n_rows = T // LANES
    assert n_rows * LANES == T and n_rows & (n_rows - 1) == 0
    shp = (B, E * n_rows, LANES)
    spec = pl.BlockSpec((1, E * n_rows, LANES), lambda b: (b, 0, 0))
    rank, rsel = pl.pallas_call(
        functools.partial(_select_kernel, cap=cap, n_rows=n_rows),
        grid=(B,),
        in_specs=[spec],
        out_specs=[spec, spec],
        out_shape=[jax.ShapeDtypeStruct(shp, I32), jax.ShapeDtypeStruct(shp, I32)],
        compiler_params=_cparams(("parallel",)),
    )(aff_et.reshape(shp))
    return rank.reshape(B, E, T), rsel.reshape(B, E, T)


def _tile_chunks(cnt_ref, base, j, cs, nk):
    lo = cnt_ref[base + j]
    hi = cnt_ref[base + j + 1]
    k0 = jnp.minimum(lo // cs, nk - 1)
    k1 = jnp.minimum(jnp.maximum(hi - 1, lo) // cs, nk - 1)
    return lo, hi, k0, k1


def _moe_ffn_kernel(cnt_ref, h_ref, rs_ref, wg_ref, wu_ref, wd_ref, ye_ref, xacc_ref, *, nt, tt, cs, nk):
    b, e, j = pl.program_id(0), pl.program_id(1), pl.program_id(2)
    n_e = pl.num_programs(1)

    @pl.when(j == 0)
    def _():
        xacc_ref[...] = jnp.zeros_like(xacc_ref)

    lo, hi, k0, k1 = _tile_chunks(cnt_ref, (b * n_e + e) * (nt + 1), j, cs, nk)
    rs = rs_ref[0, 0]
    slot = lax.broadcasted_iota(I32, (cs, tt), 0)

    def gather(kc):
        onehot = jnp.where(rs == slot + kc * cs, 1.0, 0.0).astype(BF16)
        start = pl.multiple_of(kc * cs, cs)
        xacc_ref[pl.ds(start, cs), :] += jnp.dot(onehot, h_ref[0], preferred_element_type=F32)

    @pl.when(hi > lo)
    def _():
        gather(k0)

        @pl.when(k1 > k0)
        def _():
            gather(k1)

    @pl.when(j == nt - 1)
    def _():
        xe = xacc_ref[...].astype(BF16)
        gt = jnp.dot(xe, wg_ref[0].astype(BF16), preferred_element_type=F32)
        up = jnp.dot(xe, wu_ref[0].astype(BF16), preferred_element_type=F32)
        hid = (gt * jax.nn.sigmoid(gt) * up).astype(BF16)
        ye_ref[0, 0] = jnp.dot(hid, wd_ref[0].astype(BF16), preferred_element_type=F32).astype(BF16)


def _moe_ffn(cnt, h, rsel_et, wg, wu, wd, *, row0, n_tok, cap, tt, cs):
    B, R, D = h.shape
    E, _, F = wg.shape
    nt, nk = n_tok // tt, cap // cs
    assert (nt == 1 and nk == 1) or tt <= cs
    off = row0 // tt
    assert off * tt == row0
    grid_spec = pltpu.PrefetchScalarGridSpec(
        num_scalar_prefetch=1,
        grid=(B, E, nt),
        in_specs=[
            pl.BlockSpec((1, tt, D), lambda b, e, j, c: (b, j + off, 0)),
            pl.BlockSpec((1, 1, 1, tt), lambda b, e, j, c: (b, e, 0, j)),
            pl.BlockSpec((1, D, F), lambda b, e, j, c: (e, 0, 0)),
            pl.BlockSpec((1, D, F), lambda b, e, j, c: (e, 0, 0)),
            pl.BlockSpec((1, F, D), lambda b, e, j, c: (e, 0, 0)),
        ],
        out_specs=pl.BlockSpec((1, 1, cap, D), lambda b, e, j, c: (b, e, 0, 0)),
        scratch_shapes=[pltpu.VMEM((cap, D), F32)],
    )
    return pl.pallas_call(
        functools.partial(_moe_ffn_kernel, nt=nt, tt=tt, cs=cs, nk=nk),
        grid_spec=grid_spec,
        out_shape=jax.ShapeDtypeStruct((B, E, cap, D), BF16),
        compiler_params=_cparams(("parallel", "parallel", "arbitrary")),
    )(cnt, h, rsel_et.reshape(B, E, 1, n_tok), wg, wu, wd)


def _moe_scatter_kernel(cnt_ref, *refs, nt, tt, cs, nk, mod_row, n_e):
    ye_refs = refs[:2 * n_e]
    rs_ref, aff_ref, x_ref, ng_ref, mods_ref, o_ref, yacc_ref = refs[2 * n_e:]
    b, j = pl.program_id(0), pl.program_id(1)
    yacc_ref[...] = jnp.zeros_like(yacc_ref)
    slot0 = lax.broadcasted_iota(I32, (tt, cs), 1)

    for e in range(n_e):
        lo, hi, k0, k1 = _tile_chunks(cnt_ref, (b * n_e + e) * (nt + 1), j, cs, nk)

        def add(ye_ref, kc, e=e):
            rcol = rs_ref[0, :, e:e + 1]
            onehot = jnp.where(rcol == slot0 + kc * cs, 1.0, 0.0).astype(BF16)
            yacc_ref[...] += aff_ref[0, :, e:e + 1] * jnp.dot(onehot, ye_ref[0, 0], preferred_element_type=F32)

        @pl.when(hi > lo)
        def _():
            add(ye_refs[2 * e], k0)

        @pl.when((hi > lo) & (k1 > k0))
        def _():
            add(ye_refs[2 * e + 1], k1)

    gate = mods_ref[0, mod_row, 5:6, :]
    o_ref[0] = x_ref[0] + gate * _rms(yacc_ref[...], ng_ref[...])


def _moe_scatter(cnt, ye, rsel_te, aff_te, xa, norm_g, mods, *, row0, n_tok, cap, tt, cs, mod_row, only_routed):
    B, R, D = xa.shape
    E = ye.shape[1]
    nt, nk = n_tok // tt, cap // cs
    off = row0 // tt

    def ye_map(b, j, c, *, e, second):
        _, _, k0, k1 = _tile_chunks(c, (b * E + e) * (nt + 1), j, cs, nk)
        return (b, e, k1 if second else k0, 0)

    ye_specs = [pl.BlockSpec((1, 1, cs, D), functools.partial(ye_map, e=e, second=second))
                for e in range(E) for second in (False, True)]
    tok = lambda n: pl.BlockSpec((1, tt, n), lambda b, j, c: (b, j + off, 0))
    grid_spec = pltpu.PrefetchScalarGridSpec(
        num_scalar_prefetch=1,
        grid=(B, nt),
        in_specs=ye_specs + [
            tok(E), tok(E), tok(D),
            pl.BlockSpec((1, D), lambda b, j, c: (0, 0)),
            pl.BlockSpec((1, 2, 6, D), lambda b, j, c: (b, 0, 0, 0)),
        ],
        out_specs=pl.BlockSpec((1, tt, D), lambda b, j, c: (b, j, 0)) if only_routed else tok(D),
        scratch_shapes=[pltpu.VMEM((tt, D), F32)],
    )
    return pl.pallas_call(
        functools.partial(_moe_scatter_kernel, nt=nt, tt=tt, cs=cs, nk=nk, mod_row=mod_row, n_e=E),
        grid_spec=grid_spec,
        out_shape=jax.ShapeDtypeStruct((B, n_tok if only_routed else R, D), F32),
        input_output_aliases={} if only_routed else {2 * E + 3: 0},
        compiler_params=_cparams(("parallel", "parallel")),
    )(cnt, *([ye] * (2 * E)), rsel_te, aff_te, xa, norm_g.reshape(1, D), mods)


def _moe(xa, h, aff, wg, wu, wd, norm_g, mods, *, row0, n_tok, mod_row, only_routed=False):
    B, R, D = xa.shape
    E = aff.shape[2]
    cap = EC_CAPACITY_FACTOR * n_tok // E
    if n_tok <= 256:
        tt, cs = n_tok, cap
    else:
        tt = cs = min(256, cap)
    aff_et = jnp.swapaxes(aff[:, row0:row0 + n_tok], 1, 2)
    rank, rsel = _select(aff_et, cap)
    cnt = jnp.concatenate([rank[:, :, ::tt], jnp.full((B, E, 1), cap, I32)], axis=2).reshape(-1)
    ye = _moe_ffn(cnt, h, rsel, wg, wu, wd, row0=row0, n_tok=n_tok, cap=cap, tt=tt, cs=cs)
    rsel_te = jnp.pad(jnp.swapaxes(rsel, 1, 2), ((0, 0), (row0, R - row0 - n_tok), (0, 0)), constant_values=-1)
    return _moe_scatter(cnt, ye, rsel_te, aff, xa, norm_g, mods, row0=row0, n_tok=n_tok, cap=cap, tt=tt, cs=cs,
                        mod_row=mod_row, only_routed=only_routed)


def _rope_tables(n_ctx, n_lat):
    lane = jnp.arange(LANES) % DA_QK_DIM
    axis, freq = lane // (2 * ROPE_FREQS), lane % ROPE_FREQS
    sign = jnp.where((lane % (2 * ROPE_FREQS)) < ROPE_FREQS, -1.0, 1.0)
    t = jnp.arange(n_lat)
    pos = jnp.where(axis[None, :] == 0, (t // GRID_W)[:, None], (t % GRID_W)[:, None]).astype(F32)
    ang = pos * (ROPE_BASE ** (-freq.astype(F32) / ROPE_FREQS))[None, :]
    cos = jnp.concatenate([jnp.ones((n_ctx, LANES), F32), jnp.cos(ang)], axis=0)
    sin = jnp.concatenate([jnp.zeros((n_ctx, LANES), F32), jnp.sin(ang) * sign], axis=0)
    return cos, sin


def _layout_w_in(w):
    parts, off = [], 0
    for n in IN_SIZES:
        parts.append(w[:, off:off + n])
        off += n
    wq, wk, wv, g_qkv, g_gate, g_a, g_b, r_rkv, r_xw, r_xa, r_xg = parts
    partner = jnp.arange(512) ^ ROPE_FREQS
    misc = jnp.concatenate([g_a, g_b, r_xw, r_xa, r_xg], axis=1)
    misc = jnp.pad(misc, ((0, 0), (0, WIDTH - misc.shape[1])))
    return jnp.concatenate([wq, wq[:, partner], wk, wk[:, partner], wv, g_qkv, r_rkv, g_gate, misc],
                           axis=1).astype(BF16)


def kernel(x, c, ctx, c_ctx, w_mod, b_mod, norm_pre, norm_post, w_in, w_out, da_lambda, da_norm, gdn_conv,
           gdn_a_log, gdn_dt_bias, gdn_norm, rwkv_w0, rwkv_w_up, rwkv_a0, rwkv_a_up, rwkv_g_up, rwkv_k_k,
           rwkv_k_a, rwkv_r_k, rwkv_gn_w, rwkv_gn_b, moe_router, moe_w_gate, moe_w_up, moe_w_down):
    B, T, D = x.shape
    n_ctx = ctx.shape[1]
    depth = w_mod.shape[0]
    cos, sin = _rope_tables(n_ctx, T)
    cond = jnp.concatenate([jax.nn.silu(c_ctx.astype(F32))[None], jax.nn.silu(c.astype(F32))], axis=0)
    xa = jnp.concatenate([ctx.astype(x.dtype), x], axis=1)

    for i in range(depth):
        last = i == depth - 1
        m = jnp.dot(cond, w_mod[i], precision=HI) + b_mod[i]
        mods = jnp.stack([jnp.broadcast_to(m[:1], (B, 6 * D)), m[1:]], axis=1).reshape(B, 2, 6, D)

        qa, ka, va, rest = _inproj(xa, norm_pre[i, 0], mods, _layout_w_in(w_in[i]), cos, sin, n_ctx)

        lam_init = 0.8 - 0.6 * math.exp(-0.3 * i)
        lv = da_lambda[i].astype(F32)
        lam = jnp.exp(jnp.sum(lv[0] * lv[1])) - jnp.exp(jnp.sum(lv[2] * lv[3])) + lam_init
        a_out = _attention(qa, ka, va, lam, da_norm[i], n_ctx=n_ctx, out_scale=1.0 - lam_init)

        g_f, g_b = _scan(_gdn_prep(rest, gdn_conv[i], gdn_a_log[i], gdn_dt_bias[i], n_ctx),
                         _gdn_prep_chunk, _gdn_update, n_ctx)
        rw = _rwkv_prep(rest, rwkv_w_up[i], rwkv_a_up[i], rwkv_g_up[i], rwkv_w0[i], rwkv_a0[i],
                        rwkv_k_k[i], rwkv_k_a[i], rwkv_r_k[i])
        r_f, r_b = _scan(rw, _rwkv_prep_chunk, _rwkv_update, n_ctx)

        xa = _outproj(a_out, g_f, g_b, r_f, r_b, rest, rw, xa, w_out[i].astype(BF16), gdn_norm[i],
                      rwkv_gn_w[i], rwkv_gn_b[i], norm_post[i, 0], mods, n_ctx)

        h, aff = _router(xa, norm_pre[i, 1], mods, moe_router[i], n_ctx)
        wg, wu, wd = moe_w_gate[i], moe_w_up[i], moe_w_down[i]
        xa = _moe(xa, h, aff, wg, wu, wd, norm_post[i, 1], mods, row0=n_ctx, n_tok=T, mod_row=1, only_routed=last)
        if not last:
            xa = _moe(xa, h, aff, wg, wu, wd, norm_post[i, 1], mods, row0=0, n_tok=n_ctx, mod_row=0)
    return xa
```

```python
import functools
import math

import jax
import jax.numpy as jnp
from jax import lax
from jax.experimental import pallas as pl
from jax.experimental.pallas import tpu as pltpu

F32 = jnp.float32
BF16 = jnp.bfloat16
I32 = jnp.int32
HI = lax.Precision.HIGHEST

GRID_W = 64
NORM_EPS = 1e-6
DA_HEADS = 4
DA_QK_DIM = 64
DA_V_DIM = 128
DA_WIDTH = DA_HEADS * DA_V_DIM
ROPE_BASE = 10000.0
ROPE_FREQS = DA_QK_DIM // 4
HEADS = 4
HEAD_DIM = 64
WIDTH = HEADS * HEAD_DIM
GDN_CONV = 5
CHUNK = 64
SUB_BLOCK = 16
RWKV_W_LORA = 32
RWKV_A_LORA = 32
RWKV_G_LORA = 64
RWKV_DECAY_SCALE = 0.6065306597126334
RWKV_GN_EPS = 64e-5
N_EXPERTS = 16
EC_CAPACITY_FACTOR = 2
N_DIRS = 2
IN_SIZES = (512, 512, 512, 768, 256, 8, 8, 768, 32, 32, 64)
LANES = 128
SUBLANES = 8
VMEM_LIMIT = 56 * 1024 * 1024

W_ROPE = 4 * 512
W_V = 512
W_REST = 2048
M_A, M_B, M_XW, M_XA, M_XG = 0, 8, 16, 48, 80
GDN_PACK = 7
RWKV_PACK = 11


def _pick(n, cands):
    for c in cands:
        if n % c == 0:
            return c
    raise ValueError(f"no tile for {n} in {cands}")


def _cparams(sem):
    return pltpu.CompilerParams(dimension_semantics=sem, vmem_limit_bytes=VMEM_LIMIT)


def _rms(x, g):
    return x * lax.rsqrt(jnp.mean(x * x, axis=-1, keepdims=True) + NORM_EPS) * g


def _row_mod(mods_ref, idx, row0, n_rows, n_ctx):
    rows = row0 + lax.broadcasted_iota(I32, (n_rows, 1), 0)
    return jnp.where(rows < n_ctx, mods_ref[0, 0, idx:idx + 1, :], mods_ref[0, 1, idx:idx + 1, :])


def _block(i, size):
    return pl.ds(i * size, size) if isinstance(i, int) else pl.ds(pl.multiple_of(i * size, size), size)


def _split3(x):
    hi = x.astype(BF16)
    r = x - hi.astype(F32)
    mid = r.astype(BF16)
    return hi, mid, (r - mid.astype(F32)).astype(BF16)


def _mm_exact_rhs(x, sel):
    return sum(jnp.dot(t, sel, preferred_element_type=F32) for t in _split3(x))


def _mm_hi(a, b):
    ah = a.astype(BF16)
    al = (a - ah.astype(F32)).astype(BF16)
    bh = b.astype(BF16)
    bl = (b - bh.astype(F32)).astype(BF16)
    d = lambda u, v: jnp.dot(u, v, preferred_element_type=F32)
    return d(ah, bh) + d(ah, bl) + d(al, bh)


def _head_ones():
    shift = int(math.log2(HEAD_DIM))
    r = lax.broadcasted_iota(I32, (WIDTH, WIDTH), 0) >> shift
    c = lax.broadcasted_iota(I32, (WIDTH, WIDTH), 1) >> shift
    return r == c


def _inproj_kernel(x_ref, g_ref, mods_ref, w_ref, cos_ref, sin_ref, q_ref, k_ref, v_ref, rest_ref, *, tm, n_ctx):
    row0 = pl.program_id(1) * tm
    h = _rms(x_ref[0], g_ref[...])
    h = h * (1.0 + _row_mod(mods_ref, 1, row0, tm, n_ctx)) + _row_mod(mods_ref, 0, row0, tm, n_ctx)
    hb = h.astype(BF16)
    cos = jnp.concatenate([cos_ref[...]] * 4, axis=1)
    sin = jnp.concatenate([sin_ref[...]] * 4, axis=1)
    proj = lambda a, b: jnp.dot(hb, w_ref[:, a:b], preferred_element_type=F32)
    q_ref[0] = ((proj(0, 512) * cos + proj(512, 1024) * sin) * DA_QK_DIM ** -0.5).astype(BF16)
    k_ref[0] = (proj(1024, 1536) * cos + proj(1536, 2048) * sin).astype(BF16)
    v_ref[0] = proj(W_ROPE, W_ROPE + W_V).astype(BF16)
    rest_ref[0] = proj(W_ROPE + W_V, W_ROPE + W_V + W_REST)


def _inproj(xa, g, mods, w, cos, sin, n_ctx):
    B, R, D = xa.shape
    tm = _pick(R, (384, 256, 128, 64))
    row = lambda n: pl.BlockSpec((1, tm, n), lambda b, i: (b, i, 0))
    return pl.pallas_call(
        functools.partial(_inproj_kernel, tm=tm, n_ctx=n_ctx),
        grid=(B, R // tm),
        in_specs=[
            row(D),
            pl.BlockSpec((1, D), lambda b, i: (0, 0)),
            pl.BlockSpec((1, 2, 6, D), lambda b, i: (b, 0, 0, 0)),
            pl.BlockSpec(w.shape, lambda b, i: (0, 0)),
            pl.BlockSpec((tm, LANES), lambda b, i: (i, 0)),
            pl.BlockSpec((tm, LANES), lambda b, i: (i, 0)),
        ],
        out_specs=[row(512), row(512), row(W_V), row(W_REST)],
        out_shape=[jax.ShapeDtypeStruct((B, R, 512), BF16), jax.ShapeDtypeStruct((B, R, 512), BF16),
                   jax.ShapeDtypeStruct((B, R, W_V), BF16), jax.ShapeDtypeStruct((B, R, W_REST), F32)],
        compiler_params=_cparams(("parallel", "parallel")),
    )(xa, g.reshape(1, D), mods, w, cos, sin)


def _attn_kernel(lam_ref, q_ref, k_ref, v_ref, g_ref, o_ref, *, hb, tq, tk_ctx, tk_all, n_ctx, n_rows, out_scale):
    heads = lambda x: jnp.stack([x[:, h * 128:(h + 1) * 128] for h in range(hb)], axis=0)
    q = heads(q_ref[0])
    lane = lax.broadcasted_iota(I32, q.shape, 2)
    zero = jnp.zeros_like(q)
    qq = jnp.concatenate([jnp.where(lane < DA_QK_DIM, q, zero),
                          jnp.where(lane >= DA_QK_DIM, q, zero)], axis=1)

    def attend(tk, n_kv):
        def body(c, carry):
            m, acc = carry
            k = heads(k_ref[0, _block(c, tk), :])
            v = heads(v_ref[0, _block(c, tk), :])
            v1 = jnp.concatenate([v, jnp.ones_like(v)], axis=-1)
            s = jnp.einsum('hqd,hkd->hqk', qq, k, preferred_element_type=F32)
            m_new = jnp.maximum(m, jnp.max(s, axis=-1, keepdims=True))
            p = jnp.exp((s - m_new).astype(BF16))
            acc = jnp.exp(m - m_new) * acc + jnp.einsum('hqk,hkd->hqd', p, v1, preferred_element_type=F32)
            return m_new, acc

        m0 = jnp.full((hb, 2 * tq, 1), -jnp.inf, F32)
        a0 = jnp.zeros((hb, 2 * tq, 2 * DA_V_DIM), F32)
        _, acc = lax.fori_loop(0, n_kv, body, (m0, a0))
        o = acc[:, :, :DA_V_DIM] / acc[:, :, DA_V_DIM:DA_V_DIM + 1]
        o = o[:, :tq] - lam_ref[0] * o[:, tq:]
        o = _rms(o, g_ref[...]) * out_scale
        for h in range(hb):
            o_ref[0, :, h * 128:(h + 1) * 128] = o[h]

    is_ctx = pl.program_id(2) < n_ctx // tq

    @pl.when(is_ctx)
    def _():
        attend(tk_ctx, n_ctx // tk_ctx)

    @pl.when(jnp.logical_not(is_ctx))
    def _():
        attend(tk_all, n_rows // tk_all)


def _attention(q, k, v, lam, g, *, n_ctx, out_scale):
    B, R, _ = q.shape
    hb = 4
    tq = _pick(math.gcd(n_ctx, R), (256, 128))
    tk_ctx = _pick(n_ctx, (768, 512, 256, 128))
    tk_all = _pick(R, (768, 512, 256, 128))
    return pl.pallas_call(
        functools.partial(_attn_kernel, hb=hb, tq=tq, tk_ctx=tk_ctx, tk_all=tk_all, n_ctx=n_ctx, n_rows=R,
                          out_scale=out_scale),
        grid=(B, DA_HEADS // hb, R // tq),
        in_specs=[
            pl.BlockSpec(memory_space=pltpu.SMEM),
            pl.BlockSpec((1, tq, hb * 128), lambda b, h, i: (b, i, h)),
            pl.BlockSpec((1, R, hb * 128), lambda b, h, i: (b, 0, h)),
            pl.BlockSpec((1, R, hb * 128), lambda b, h, i: (b, 0, h)),
            pl.BlockSpec((1, DA_V_DIM), lambda b, h, i: (0, 0)),
        ],
        out_specs=pl.BlockSpec((1, tq, hb * 128), lambda b, h, i: (b, i, h)),
        out_shape=jax.ShapeDtypeStruct((B, R, DA_WIDTH), F32),
        compiler_params=_cparams(("parallel", "parallel", "parallel")),
    )(lam.reshape(1).astype(F32), q, k, v, g.reshape(1, DA_V_DIM))


def _halo_shift(cur, prev8, next8, s, first, last):
    n = cur.shape[0]
    if s == 0:
        return cur
    row = lax.broadcasted_iota(I32, (n, 1), 0)
    rolled = pltpu.roll(cur, (-s) % n, 0)
    if s < 0:
        halo = jnp.where(first, 0.0, pltpu.roll(prev8, -s, 0))
        return jnp.where(row < -s, jnp.concatenate([halo] * (n // SUBLANES), axis=0), rolled)
    halo = jnp.where(last, 0.0, pltpu.roll(next8, SUBLANES - s, 0))
    return jnp.where(row >= n - s, jnp.concatenate([halo] * (n // SUBLANES), axis=0), rolled)


def _gdn_prep_kernel(prev_ref, cur_ref, next_ref, misc_ref, cw_ref, alog_ref, dtb_ref, o_ref, *, tp, n_ctx, n_rows):
    row0 = pl.program_id(1) * tp
    first = (row0 == 0) | (row0 == n_ctx)
    last = (row0 + tp == n_ctx) | (row0 + tp == n_rows)
    cur, prev8, next8 = cur_ref[0], prev_ref[0], next_ref[0]
    pad = GDN_CONV // 2
    y = sum(_halo_shift(cur, prev8, next8, j - pad, first, last) * cw_ref[j:j + 1, :] for j in range(GDN_CONV))
    y = y * jax.nn.sigmoid(y)
    q, k, v = y[:, :WIDTH], y[:, WIDTH:2 * WIDTH], y[:, 2 * WIDTH:]
    same_head = jnp.where(_head_ones(), 1.0, 0.0).astype(BF16)
    o_ref[0, :, 0:WIDTH] = q * lax.rsqrt(_mm_exact_rhs(q * q, same_head) + 1e-12) * HEAD_DIM ** -0.5
    o_ref[0, :, WIDTH:2 * WIDTH] = k * lax.rsqrt(_mm_exact_rhs(k * k, same_head) + 1e-12)
    o_ref[0, :, 2 * WIDTH:3 * WIDTH] = v
    m = misc_ref[0]
    x = m + dtb_ref[...]
    softplus = jnp.maximum(x, 0.0) + jnp.log1p(jnp.exp(-jnp.abs(x)))
    g_all = -jnp.exp(alog_ref[...]) * softplus
    b_all = jax.nn.sigmoid(m)
    src = lax.broadcasted_iota(I32, (WIDTH, WIDTH), 0)
    head = lax.broadcasted_iota(I32, (WIDTH, WIDTH), 1) >> int(math.log2(HEAD_DIM))
    for d in range(N_DIRS):
        pick_g = jnp.where(src == M_A + d * HEADS + head, 1.0, 0.0).astype(BF16)
        pick_b = jnp.where(src == M_B + d * HEADS + head, 1.0, 0.0).astype(BF16)
        base = (3 + 2 * d) * WIDTH
        o_ref[0, :, base:base + WIDTH] = _mm_exact_rhs(g_all, pick_g)
        o_ref[0, :, base + WIDTH:base + 2 * WIDTH] = _mm_exact_rhs(b_all, pick_b)


def _halo_specs(tp, width, col_block, n_rows):
    r8 = tp // SUBLANES
    n8 = n_rows // SUBLANES
    return [
        pl.BlockSpec((1, SUBLANES, width), lambda b, i: (b, jnp.maximum(i * r8 - 1, 0), col_block)),
        pl.BlockSpec((1, tp, width), lambda b, i: (b, i, col_block)),
        pl.BlockSpec((1, SUBLANES, width), lambda b, i: (b, jnp.minimum((i + 1) * r8, n8 - 1), col_block)),
    ]


def _gdn_prep(rest, conv_w, a_log, dt_bias, n_ctx):
    B, R, _ = rest.shape
    tp = _pick(math.gcd(R, n_ctx), (256, 128, 64))
    alog = jnp.zeros((1, WIDTH), F32).at[0, M_A:M_A + N_DIRS * HEADS].set(a_log.reshape(-1))
    dtb = jnp.zeros((1, WIDTH), F32).at[0, M_A:M_A + N_DIRS * HEADS].set(dt_bias.reshape(-1))
    vec = lambda n: pl.BlockSpec((1, n), lambda b, i: (0, 0))
    return pl.pallas_call(
        functools.partial(_gdn_prep_kernel, tp=tp, n_ctx=n_ctx, n_rows=R),
        grid=(B, R // tp),
        in_specs=_halo_specs(tp, 3 * WIDTH, 0, R) + [
            pl.BlockSpec((1, tp, WIDTH), lambda b, i: (b, i, 7)),
            pl.BlockSpec((GDN_CONV, 3 * WIDTH), lambda b, i: (0, 0)),
            vec(WIDTH), vec(WIDTH)],
        out_specs=pl.BlockSpec((1, tp, GDN_PACK * WIDTH), lambda b, i: (b, i, 0)),
        out_shape=jax.ShapeDtypeStruct((B, R, GDN_PACK * WIDTH), F32),
        compiler_params=_cparams(("parallel", "parallel")),
    )(rest, rest, rest, rest, conv_w, alog, dtb)


def _rwkv_prep_kernel(rkv_ref, misc_ref, wup_ref, vec_ref, o_ref):
    rkv, m = rkv_ref[0], misc_ref[0]
    r, k, v = rkv[:, :WIDTH], rkv[:, WIDTH:2 * WIDTH], rkv[:, 2 * WIDTH:]
    k_k, k_a, r_k = vec_ref[0:1, :], vec_ref[1:2, :], vec_ref[2:3, :]
    same_head = jnp.where(_head_ones(), 1.0, 0.0).astype(BF16)
    kkr = k * k_k
    kk = kkr * lax.rsqrt(_mm_exact_rhs(kkr * kkr, same_head) + 1e-12)
    tanh_m, sig_m = jnp.tanh(m), jax.nn.sigmoid(m)
    o_ref[0, :, 0:WIDTH] = r
    o_ref[0, :, WIDTH:2 * WIDTH] = v
    o_ref[0, :, 2 * WIDTH:3 * WIDTH] = kk
    kd_sum = jnp.zeros_like(k)
    for d in range(N_DIRS):
        w0, a0 = vec_ref[3 + 2 * d:4 + 2 * d, :], vec_ref[4 + 2 * d:5 + 2 * d, :]
        logw = -RWKV_DECAY_SCALE * jax.nn.sigmoid(w0 + _mm_hi(tanh_m, wup_ref[2 * d]))
        a_d = jax.nn.sigmoid(a0 + _mm_hi(m, wup_ref[2 * d + 1]))
        kd = k * (1.0 + (a_d - 1.0) * k_a)
        kd_sum = kd_sum + kd
        base = (3 + 3 * d) * WIDTH
        o_ref[0, :, base:base + WIDTH] = kd
        o_ref[0, :, base + WIDTH:base + 2 * WIDTH] = kk * a_d
        o_ref[0, :, base + 2 * WIDTH:base + 3 * WIDTH] = logw
    o_ref[0, :, 9 * WIDTH:10 * WIDTH] = _mm_hi(sig_m, wup_ref[2 * N_DIRS])
    o_ref[0, :, 10 * WIDTH:11 * WIDTH] = _mm_exact_rhs(r * kd_sum * r_k, same_head) * v


def _rwkv_prep(rest, w_up, a_up, g_up, w0, a0, k_k, k_a, r_k):
    B, R, _ = rest.shape
    tp = _pick(R, (256, 128, 64))
    embed = lambda w, lane0: jnp.zeros((WIDTH, WIDTH), F32).at[lane0:lane0 + w.shape[0]].set(w)
    wup = jnp.stack([embed(w_up[0], M_XW), embed(a_up[0], M_XA), embed(w_up[1], M_XW), embed(a_up[1], M_XA),
                     embed(g_up, M_XG)])
    vecs = jnp.stack([k_k, k_a, r_k.reshape(-1), w0[0], a0[0], w0[1], a0[1], jnp.zeros_like(k_k)])
    return pl.pallas_call(
        _rwkv_prep_kernel,
        grid=(B, R // tp),
        in_specs=[pl.BlockSpec((1, tp, 3 * WIDTH), lambda b, i: (b, i, 1)),
                  pl.BlockSpec((1, tp, WIDTH), lambda b, i: (b, i, 7)),
                  pl.BlockSpec(wup.shape, lambda b, i: (0, 0, 0)),
                  pl.BlockSpec(vecs.shape, lambda b, i: (0, 0))],
        out_specs=pl.BlockSpec((1, tp, RWKV_PACK * WIDTH), lambda b, i: (b, i, 0)),
        out_shape=jax.ShapeDtypeStruct((B, R, RWKV_PACK * WIDTH), F32),
        compiler_params=_cparams(("parallel", "parallel")),
    )(rest, rest, wup, vecs)


def _bmm(a, b):
    return jnp.einsum('nij,njk->nik', a.astype(BF16), b.astype(BF16), preferred_element_type=F32)


def _bmm_nt(a, b):
    return jnp.einsum('nik,njk->nij', a.astype(BF16), b.astype(BF16), preferred_element_type=F32)


def _bmm_tn(a, b):
    return jnp.einsum('nki,nkj->nij', a.astype(BF16), b.astype(BF16), preferred_element_type=F32)


def _bmm_exact_lhs(sel, x):
    return sum(jnp.einsum('nij,njk->nik', sel, t, preferred_element_type=F32) for t in _split3(x))


def _scan_consts(n_batch, n_groups):
    n = N_DIRS * n_batch * n_groups
    wide = (n, CHUNK, HEADS * CHUNK)
    sq = (n, CHUNK, CHUNK)
    it = lambda shp, ax: lax.broadcasted_iota(I32, shp, ax)

    def sign(shp):
        idx = it(shp, 0)
        bwd = functools.reduce(jnp.logical_or, [(idx >= (2 * g + 1) * n_batch) & (idx < (2 * g + 2) * n_batch)
                                                for g in range(n_groups)])
        return jnp.where(bwd, -1, 1)

    delta = (it(wide, 1) - (it(wide, 2) & (CHUNK - 1))) * sign(wide)
    dsq = (it(sq, 1) - it(sq, 2)) * sign(sq)
    blk = int(math.log2(SUB_BLOCK))
    near = (it(wide, 1) >> blk) == ((it(wide, 2) & (CHUNK - 1)) >> blk)
    return dict(incl=delta >= 0, strict=delta > 0, eye=delta == 0, near=near,
                tri=jnp.where(dsq >= 0, 1.0, 0.0).astype(BF16), bd=_head_ones()[None])


def _bd(x, c):
    return jnp.where(c['bd'], jnp.concatenate([x] * HEADS, axis=1), 0.0)


def _unit_tri_inverse(a, c):
    eye = jnp.where(c['eye'], 1.0, 0.0)
    mul = lambda x, y: _bmm(x, _bd(y, c))
    d = jnp.where(c['near'], a, 0.0)
    n = -d
    t_d = eye + n
    for _ in range(int(math.log2(SUB_BLOCK)) - 1):
        n = mul(n, n)
        t_d = t_d + mul(t_d, n)
    m = mul(t_d, a - d)
    p = (eye - m) + mul(eye - m, mul(m, m))
    return mul(p, t_d)


def _chunk_rev(s, n_ctx_chunks, n_chunks):
    return jnp.where(s < n_ctx_chunks, n_ctx_chunks - 1 - s, n_chunks - 1 - (s - n_ctx_chunks))


def _gdn_prep_chunk(col, c):
    q, k, v, gw, bw = col(0, 0), col(1, 1), col(2, 2), col(3, 5), col(4, 6)
    gc = _bmm_exact_lhs(c['tri'], gw)
    ones = jnp.ones(c['tri'].shape, BF16)
    gr = _bmm_exact_lhs(ones, jnp.where(c['eye'], gc, 0.0))
    decay = jnp.exp(jnp.where(c['incl'], gc - gr, -jnp.inf))
    kb = k * bw
    km = _bd(k, c)
    a_mat = jnp.where(c['strict'], _bmm_nt(kb, km) * decay, 0.0)
    qk = _bmm_nt(q, km) * decay
    t_inv = _unit_tri_inverse(a_mat, c)
    uw = _bmm(t_inv, jnp.concatenate([_bd(v * bw, c), _bd(kb * jnp.exp(gc), c)], axis=2))
    u, w = uw[:, :, :WIDTH], uw[:, :, WIDTH:]
    g_end = jnp.min(gc, axis=1, keepdims=True)
    return dict(u=u, w=w, q_dec=q * jnp.exp(gc), qk=qk, k_dec=k * jnp.exp(g_end - gc), keep=jnp.exp(g_end))


def _gdn_update(p, c, s):
    v_new = p['u'] - _bmm(p['w'], s)
    o = _bmm(p['q_dec'], s) + _bmm(p['qk'], _bd(v_new, c))
    return o, s * p['keep'] + jnp.where(c['bd'], _bmm_tn(p['k_dec'], v_new), 0.0)


def _rwkv_prep_chunk(col, c):
    r, v, kk, kd, b, lw = col(0, 0), col(1, 1), col(2, 2), col(3, 6), col(4, 7), col(5, 8)
    n4 = HEADS * CHUNK
    cum = _bmm_exact_lhs(c['tri'], lw)
    inv = jnp.exp(-cum)
    c_s = kk * jnp.exp(cum - lw)
    r_s = r * jnp.exp(cum)
    big = _bmm_nt(jnp.concatenate([c_s, r_s], axis=1),
                  jnp.concatenate([_bd(b * inv, c), _bd(kd * inv, c)], axis=1))
    l_cb = jnp.where(c['strict'], big[:, :CHUNK, :n4], 0.0)
    l_ck = jnp.where(c['strict'], big[:, :CHUNK, n4:], 0.0)
    a_rb = jnp.where(c['incl'], big[:, CHUNK:, :n4], 0.0)
    a_rk = jnp.where(c['incl'], big[:, CHUNK:, n4:], 0.0)
    t_inv = _unit_tri_inverse(l_cb, c)
    lv = _bmm(jnp.concatenate([l_ck, a_rk], axis=1), _bd(v, c))
    uc = _bmm(t_inv, jnp.concatenate([_bd(lv[:, :CHUNK], c), _bd(c_s, c)], axis=2))
    u1, cw = uc[:, :, :WIDTH], uc[:, :, WIDTH:]
    end = jnp.min(cum, axis=1, keepdims=True)
    e_end = jnp.exp(end - cum)
    return dict(u1=u1, cw_r=jnp.concatenate([cw, r_s], axis=1), y1=lv[:, CHUNK:], a_rb=a_rb, v=v,
                kb_end=jnp.concatenate([kd * e_end, -(b * e_end)], axis=1), keep=jnp.exp(end))


def _rwkv_update(p, c, st):
    su = _bmm_nt(p['cw_r'], st)
    u = p['u1'] + su[:, :CHUNK]
    y = p['y1'] + su[:, CHUNK:] - _bmm(p['a_rb'], _bd(u, c))
    upd = _bmm_tn(jnp.concatenate([p['v'], u], axis=1), p['kb_end'])
    return y, st * p['keep'] + jnp.where(c['bd'], upd, 0.0)


SCAN_SUB = 4


def _scan_kernel(f_ref, b_ref, of_ref, ob_ref, s_ref, *, prep, update):
    @pl.when(pl.program_id(0) == 0)
    def _():
        s_ref[...] = jnp.zeros_like(s_ref)

    n_batch = f_ref.shape[0]
    rows = lambda i: slice(i * CHUNK, (i + 1) * CHUNK)
    grp = lambda ref, i, j: ref[:, rows(i), j * WIDTH:(j + 1) * WIDTH]
    col = lambda jf, jb: jnp.concatenate(
        [x for i in range(SCAN_SUB) for x in (grp(f_ref, i, jf), grp(b_ref, SCAN_SUB - 1 - i, jb))], axis=0)
    c = _scan_consts(n_batch, SCAN_SUB)
    prepared = prep(col, c)
    n = N_DIRS * n_batch
    s = s_ref[...]
    for i in range(SCAN_SUB):
        o, s = update({k: v[i * n:(i + 1) * n] for k, v in prepared.items()}, c, s)
        of_ref[:, rows(i), :] = o[:n_batch]
        ob_ref[:, rows(SCAN_SUB - 1 - i), :] = o[n_batch:]
    s_ref[...] = s


def _scan(packed, prep, update, n_ctx):
    B, R, P = packed.shape
    rows = SCAN_SUB * CHUNK
    assert R % rows == 0 and n_ctx % rows == 0
    rev = functools.partial(_chunk_rev, n_ctx_chunks=n_ctx // rows, n_chunks=R // rows)
    out = jax.ShapeDtypeStruct((B, R, WIDTH), F32)
    return pl.pallas_call(
        functools.partial(_scan_kernel, prep=prep, update=update),
        grid=(R // rows,),
        in_specs=[pl.BlockSpec((B, rows, P), lambda s: (0, s, 0)),
                  pl.BlockSpec((B, rows, P), lambda s: (0, rev(s), 0))],
        out_specs=[pl.BlockSpec((B, rows, WIDTH), lambda s: (0, s, 0)),
                   pl.BlockSpec((B, rows, WIDTH), lambda s: (0, rev(s), 0))],
        out_shape=[out, out],
        scratch_shapes=[pltpu.VMEM((B * N_DIRS, WIDTH, WIDTH), F32)],
        compiler_params=_cparams(("arbitrary",)),
    )(packed, packed)


def _outproj_kernel(a_ref, gf_ref, gb_ref, gg_ref, rf_ref, rb_ref, rg_ref, rbn_ref, x_ref, w_ref, vec_ref, ng_ref,
                    mods_ref, o_ref, *, tm, n_ctx):
    row0 = pl.program_id(1) * tm
    same_head = jnp.where(_head_ones(), 1.0, 0.0).astype(BF16)
    hmean = lambda t: _mm_exact_rhs(t, same_head) * (1.0 / HEAD_DIM)
    og = gf_ref[0] + gb_ref[0]
    gate = gg_ref[0]
    y_g = og * lax.rsqrt(hmean(og * og) + NORM_EPS) * vec_ref[0:1, :] * (gate * jax.nn.sigmoid(gate))
    yr = rf_ref[0] + rb_ref[0]
    dev = yr - hmean(yr)
    yn = dev * lax.rsqrt(hmean(dev * dev) + RWKV_GN_EPS) * vec_ref[1:2, :] + vec_ref[2:3, :]
    y_r = (yn + rbn_ref[0]) * rg_ref[0]
    mix = jnp.dot(a_ref[0].astype(BF16), w_ref[0:DA_WIDTH, :], preferred_element_type=F32)
    mix += jnp.dot(y_g.astype(BF16), w_ref[DA_WIDTH:DA_WIDTH + WIDTH, :], preferred_element_type=F32)
    mix += jnp.dot(y_r.astype(BF16), w_ref[DA_WIDTH + WIDTH:, :], preferred_element_type=F32)
    o_ref[0] = x_ref[0] + _row_mod(mods_ref, 2, row0, tm, n_ctx) * _rms(mix, ng_ref[...])


def _outproj(a, g_f, g_b, r_f, r_b, rest, rw, xa, w, gdn_norm, gn_w, gn_b, norm_g, mods, n_ctx):
    B, R, D = xa.shape
    tm = _pick(R, (384, 256, 128, 64))
    vecs = jnp.stack([jnp.tile(gdn_norm, HEADS), gn_w, gn_b, jnp.zeros_like(gn_w)] + [jnp.zeros_like(gn_w)] * 4)
    row = lambda n, cb=0: pl.BlockSpec((1, tm, n), lambda b, i: (b, i, cb))
    return pl.pallas_call(
        functools.partial(_outproj_kernel, tm=tm, n_ctx=n_ctx),
        grid=(B, R // tm),
        in_specs=[row(DA_WIDTH), row(WIDTH), row(WIDTH), row(WIDTH, 6), row(WIDTH), row(WIDTH),
                  row(WIDTH, 9), row(WIDTH, 10), row(D),
                  pl.BlockSpec(w.shape, lambda b, i: (0, 0)),
                  pl.BlockSpec(vecs.shape, lambda b, i: (0, 0)),
                  pl.BlockSpec((1, D), lambda b, i: (0, 0)),
                  pl.BlockSpec((1, 2, 6, D), lambda b, i: (b, 0, 0, 0))],
        out_specs=row(D),
        out_shape=jax.ShapeDtypeStruct((B, R, D), F32),
        compiler_params=_cparams(("parallel", "parallel")),
    )(a, g_f, g_b, rest, r_f, r_b, rw, rw, xa, w, vecs, norm_g.reshape(1, D), mods)


def _router_kernel(x_ref, g_ref, mods_ref, wr_ref, h_ref, aff_ref, *, tm, n_ctx):
    row0 = pl.program_id(1) * tm
    h = _rms(x_ref[0], g_ref[...])
    h = h * (1.0 + _row_mod(mods_ref, 4, row0, tm, n_ctx)) + _row_mod(mods_ref, 3, row0, tm, n_ctx)
    h_ref[0] = h.astype(BF16)
    logits = jnp.dot(h, wr_ref[...], precision=HI, preferred_element_type=F32)
    e = jnp.exp(logits - jnp.max(logits, axis=-1, keepdims=True))
    aff_ref[0] = e / jnp.sum(e, axis=-1, keepdims=True)


def _router(xa, g, mods, wr, n_ctx):
    B, R, D = xa.shape
    E = wr.shape[1]
    tm = _pick(R, (768, 512, 256, 128, 64))
    return pl.pallas_call(
        functools.partial(_router_kernel, tm=tm, n_ctx=n_ctx),
        grid=(B, R // tm),
        in_specs=[pl.BlockSpec((1, tm, D), lambda b, i: (b, i, 0)),
                  pl.BlockSpec((1, D), lambda b, i: (0, 0)),
                  pl.BlockSpec((1, 2, 6, D), lambda b, i: (b, 0, 0, 0)),
                  pl.BlockSpec((D, E), lambda b, i: (0, 0))],
        out_specs=[pl.BlockSpec((1, tm, D), lambda b, i: (b, i, 0)),
                   pl.BlockSpec((1, tm, E), lambda b, i: (b, i, 0))],
        out_shape=[jax.ShapeDtypeStruct((B, R, D), BF16), jax.ShapeDtypeStruct((B, R, E), F32)],
        compiler_params=_cparams(("parallel", "parallel")),
    )(xa, g.reshape(1, D), mods, wr)


def _select_kernel(a_ref, rank_ref, rsel_ref, *, cap, n_rows):
    a = a_ref[0]
    rows = a.shape[0]
    ri = lax.broadcasted_iota(I32, (rows, rows), 0)
    ci = lax.broadcasted_iota(I32, (rows, rows), 1)
    shift = int(math.log2(n_rows))
    same = (ri >> shift) == (ci >> shift)
    same_e = jnp.where(same, 1.0, 0.0).astype(BF16)
    before_e = jnp.where(same & (ci < ri), 1.0, 0.0).astype(BF16)
    li = lax.broadcasted_iota(I32, (LANES, LANES), 0)
    lj = lax.broadcasted_iota(I32, (LANES, LANES), 1)
    upper = jnp.where(li <= lj, 1.0, 0.0).astype(BF16)
    ones = jnp.ones((LANES, LANES), BF16)

    def row_tot(m):
        return jnp.dot(m, ones, preferred_element_type=F32).astype(BF16)

    def count(mask):
        return jnp.dot(same_e, row_tot(jnp.where(mask, 1.0, 0.0).astype(BF16)), preferred_element_type=F32)

    def prefix(mask):
        m = jnp.where(mask, 1.0, 0.0).astype(BF16)
        incl = jnp.dot(m, upper, preferred_element_type=F32)
        off = jnp.dot(before_e, row_tot(m), preferred_element_type=F32)
        return incl - m.astype(F32) + off

    def body(it, tau):
        cand = tau | jnp.left_shift(jnp.int32(1), 30 - it)
        return jnp.where(count(a >= pltpu.bitcast(cand, F32)) >= cap, cand, tau)

    tau = lax.fori_loop(0, 31, body, jnp.zeros(a.shape, I32))
    sure = a >= pltpu.bitcast(tau + 1, F32)
    band = jnp.logical_and(a >= pltpu.bitcast(tau, F32), jnp.logical_not(sure))
    need = cap - count(sure)
    sel = jnp.logical_or(sure, jnp.logical_and(band, prefix(band) < need))
    rank = prefix(sel).astype(I32)
    rank_ref[0] = rank
    rsel_ref[0] = jnp.where(sel, rank, -1)


def _select(aff_et, cap):
    B, E, T = aff_et.shape
    n_rows = T // LANES
    assert n_rows * LANES == T and n_rows & (n_rows - 1) == 0
    shp = (B, E * n_rows, LANES)
    spec = pl.BlockSpec((1, E * n_rows, LANES), lambda b: (b, 0, 0))
    rank, rsel = pl.pallas_call(
        functools.partial(_select_kernel, cap=cap, n_rows=n_rows),
        grid=(B,),
        in_specs=[spec],
        out_specs=[spec, spec],
        out_shape=[jax.ShapeDtypeStruct(shp, I32), jax.ShapeDtypeStruct(shp, I32)],
        compiler_params=_cparams(("parallel",)),
    )(aff_et.reshape(shp))
    return rank.reshape(B, E, T), rsel.reshape(B, E, T)


def _tile_chunks(cnt_ref, base, j, cs, nk):
    lo = cnt_ref[base + j]
    hi = cnt_ref[base + j + 1]
    k0 = jnp.minimum(lo // cs, nk - 1)
    k1 = jnp.minimum(jnp.maximum(hi - 1, lo) // cs, nk - 1)
    return lo, hi, k0, k1


def _span(tt, cs, nk):
    return min(nk, -(-tt // cs) + 1)


def _moe_ffn_kernel(cnt_ref, h_ref, rs_ref, wg_ref, wu_ref, wd_ref, ye_ref, xacc_ref, *, nt, tt, cs, nk):
    b, e, j = pl.program_id(0), pl.program_id(1), pl.program_id(2)
    n_e = pl.num_programs(1)

    @pl.when(j == 0)
    def _():
        xacc_ref[...] = jnp.zeros_like(xacc_ref)

    lo, hi, k0, k1 = _tile_chunks(cnt_ref, (b * n_e + e) * (nt + 1), j, cs, nk)
    rs = rs_ref[0, 0]
    slot = lax.broadcasted_iota(I32, (cs, tt), 0)

    def gather(kc):
        onehot = jnp.where(rs == slot + kc * cs, 1.0, 0.0).astype(BF16)
        start = pl.multiple_of(kc * cs, cs)
        xacc_ref[pl.ds(start, cs), :] += jnp.dot(onehot, h_ref[0], preferred_element_type=F32)

    for i in range(_span(tt, cs, nk)):
        @pl.when((hi > lo) & (k0 + i <= k1))
        def _():
            gather(k0 + i)

    @pl.when(j == nt - 1)
    def _():
        xe = xacc_ref[...].astype(BF16)
        gt = jnp.dot(xe, wg_ref[0, 0].astype(BF16), preferred_element_type=F32)
        up = jnp.dot(xe, wu_ref[0, 0].astype(BF16), preferred_element_type=F32)
        hid = (gt * jax.nn.sigmoid(gt) * up).astype(BF16)
        ye_ref[0, 0] = jnp.dot(hid, wd_ref[0, 0].astype(BF16), preferred_element_type=F32).astype(BF16)


def _moe_ffn(cnt, h, rsel_et, wg, wu, wd, *, layer, row0, n_tok, cap, tt, cs):
    B, R, D = h.shape
    _, E, _, F = wg.shape
    nt, nk = n_tok // tt, cap // cs
    off = row0 // tt
    assert off * tt == row0
    grid_spec = pltpu.PrefetchScalarGridSpec(
        num_scalar_prefetch=1,
        grid=(B, E, nt),
        in_specs=[
            pl.BlockSpec((1, tt, D), lambda b, e, j, c: (b, j + off, 0)),
            pl.BlockSpec((1, 1, 1, tt), lambda b, e, j, c: (b, e, 0, j)),
            pl.BlockSpec((1, 1, D, F), lambda b, e, j, c: (layer, e, 0, 0)),
            pl.BlockSpec((1, 1, D, F), lambda b, e, j, c: (layer, e, 0, 0)),
            pl.BlockSpec((1, 1, F, D), lambda b, e, j, c: (layer, e, 0, 0)),
        ],
        out_specs=pl.BlockSpec((1, 1, cap, D), lambda b, e, j, c: (b, e, 0, 0)),
        scratch_shapes=[pltpu.VMEM((cap, D), F32)],
    )
    return pl.pallas_call(
        functools.partial(_moe_ffn_kernel, nt=nt, tt=tt, cs=cs, nk=nk),
        grid_spec=grid_spec,
        out_shape=jax.ShapeDtypeStruct((B, E, cap, D), BF16),
        compiler_params=_cparams(("parallel", "parallel", "arbitrary")),
    )(cnt, h, rsel_et.reshape(B, E, 1, n_tok), wg, wu, wd)


def _moe_scatter_kernel(cnt_ref, *refs, nt, tt, cs, nk, mod_row, n_e):
    ye_refs = refs[:2 * n_e]
    rs_ref, aff_ref, x_ref, ng_ref, mods_ref, o_ref, yacc_ref = refs[2 * n_e:]
    b, j = pl.program_id(0), pl.program_id(1)
    yacc_ref[...] = jnp.zeros_like(yacc_ref)
    slot0 = lax.broadcasted_iota(I32, (tt, cs), 1)

    for e in range(n_e):
        lo, hi, k0, k1 = _tile_chunks(cnt_ref, (b * n_e + e) * (nt + 1), j, cs, nk)

        def add(ye_ref, kc, e=e):
            rcol = rs_ref[0, :, e:e + 1]
            onehot = jnp.where(rcol == slot0 + kc * cs, 1.0, 0.0).astype(BF16)
            yacc_ref[...] += aff_ref[0, :, e:e + 1] * jnp.dot(onehot, ye_ref[0, 0], preferred_element_type=F32)

        @pl.when(hi > lo)
        def _():
            add(ye_refs[2 * e], k0)

        @pl.when((hi > lo) & (k1 > k0))
        def _():
            add(ye_refs[2 * e + 1], k1)

    gate = mods_ref[0, mod_row, 5:6, :]
    o_ref[0] = x_ref[0] + gate * _rms(yacc_ref[...], ng_ref[...])


def _moe_scatter(cnt, ye, rsel_te, aff_te, xa, norm_g, mods, *, row0, n_tok, cap, tt, cs, mod_row, only_routed):
    B, R, D = xa.shape
    E = ye.shape[1]
    nt, nk = n_tok // tt, cap // cs
    off = row0 // tt

    def ye_map(b, j, c, *, e, second):
        _, _, k0, k1 = _tile_chunks(c, (b * E + e) * (nt + 1), j, cs, nk)
        return (b, e, k1 if second else k0, 0)

    ye_specs = [pl.BlockSpec((1, 1, cs, D), functools.partial(ye_map, e=e, second=second))
                for e in range(E) for second in (False, True)]
    tok = lambda n: pl.BlockSpec((1, tt, n), lambda b, j, c: (b, j + off, 0))
    grid_spec = pltpu.PrefetchScalarGridSpec(
        num_scalar_prefetch=1,
        grid=(B, nt),
        in_specs=ye_specs + [
            tok(E), tok(E), tok(D),
            pl.BlockSpec((1, D), lambda b, j, c: (0, 0)),
            pl.BlockSpec((1, 2, 6, D), lambda b, j, c: (b, 0, 0, 0)),
        ],
        out_specs=pl.BlockSpec((1, tt, D), lambda b, j, c: (b, j, 0)) if only_routed else tok(D),
        scratch_shapes=[pltpu.VMEM((tt, D), F32)],
    )
    return pl.pallas_call(
        functools.partial(_moe_scatter_kernel, nt=nt, tt=tt, cs=cs, nk=nk, mod_row=mod_row, n_e=E),
        grid_spec=grid_spec,
        out_shape=jax.ShapeDtypeStruct((B, n_tok if only_routed else R, D), F32),
        input_output_aliases={} if only_routed else {2 * E + 3: 0},
        compiler_params=_cparams(("parallel", "parallel")),
    )(cnt, *([ye] * (2 * E)), rsel_te, aff_te, xa, norm_g.reshape(1, D), mods)


def _moe(xa, h, aff, wg, wu, wd, norm_g, mods, *, layer, row0, n_tok, mod_row, only_routed=False):
    B, R, D = xa.shape
    E = aff.shape[2]
    cap = EC_CAPACITY_FACTOR * n_tok // E
    if n_tok <= 256:
        tt, cs = n_tok, cap
    else:
        tt = cs = min(256, cap)
    cs_g = cs // 2 if cs >= 256 else cs
    aff_et = jnp.swapaxes(aff[:, row0:row0 + n_tok], 1, 2)
    rank, rsel = _select(aff_et, cap)
    cnt = jnp.concatenate([rank[:, :, ::tt], jnp.full((B, E, 1), cap, I32)], axis=2).reshape(-1)
    ye = _moe_ffn(cnt, h, rsel, wg, wu, wd, layer=layer, row0=row0, n_tok=n_tok, cap=cap, tt=tt, cs=cs_g)
    rsel_te = jnp.pad(jnp.swapaxes(rsel, 1, 2), ((0, 0), (row0, R - row0 - n_tok), (0, 0)), constant_values=-1)
    return _moe_scatter(cnt, ye, rsel_te, aff, xa, norm_g, mods, row0=row0, n_tok=n_tok, cap=cap, tt=tt, cs=cs,
                        mod_row=mod_row, only_routed=only_routed)


def _rope_tables(n_ctx, n_lat):
    lane = jnp.arange(LANES) % DA_QK_DIM
    axis, freq = lane // (2 * ROPE_FREQS), lane % ROPE_FREQS
    sign = jnp.where((lane % (2 * ROPE_FREQS)) < ROPE_FREQS, -1.0, 1.0)
    t = jnp.arange(n_lat)
    pos = jnp.where(axis[None, :] == 0, (t // GRID_W)[:, None], (t % GRID_W)[:, None]).astype(F32)
    ang = pos * (ROPE_BASE ** (-freq.astype(F32) / ROPE_FREQS))[None, :]
    cos = jnp.concatenate([jnp.ones((n_ctx, LANES), F32), jnp.cos(ang)], axis=0)
    sin = jnp.concatenate([jnp.zeros((n_ctx, LANES), F32), jnp.sin(ang) * sign], axis=0)
    return cos, sin


def _layout_w_in(w):
    parts, off = [], 0
    for n in IN_SIZES:
        parts.append(w[:, off:off + n])
        off += n
    wq, wk, wv, g_qkv, g_gate, g_a, g_b, r_rkv, r_xw, r_xa, r_xg = parts
    partner = jnp.arange(512) ^ ROPE_FREQS
    misc = jnp.concatenate([g_a, g_b, r_xw, r_xa, r_xg], axis=1)
    misc = jnp.pad(misc, ((0, 0), (0, WIDTH - misc.shape[1])))
    return jnp.concatenate([wq, wq[:, partner], wk, wk[:, partner], wv, g_qkv, r_rkv, g_gate, misc],
                           axis=1).astype(BF16)


def kernel(x, c, ctx, c_ctx, w_mod, b_mod, norm_pre, norm_post, w_in, w_out, da_lambda, da_norm, gdn_conv,
           gdn_a_log, gdn_dt_bias, gdn_norm, rwkv_w0, rwkv_w_up, rwkv_a0, rwkv_a_up, rwkv_g_up, rwkv_k_k,
           rwkv_k_a, rwkv_r_k, rwkv_gn_w, rwkv_gn_b, moe_router, moe_w_gate, moe_w_up, moe_w_down):
    B, T, D = x.shape
    n_ctx = ctx.shape[1]
    depth = w_mod.shape[0]
    cos, sin = _rope_tables(n_ctx, T)
    cond = jnp.concatenate([jax.nn.silu(c_ctx.astype(F32))[None], jax.nn.silu(c.astype(F32))], axis=0)
    xa = jnp.concatenate([ctx.astype(x.dtype), x], axis=1)

    for i in range(depth):
        last = i == depth - 1
        m = jnp.dot(cond, w_mod[i], precision=HI) + b_mod[i]
        mods = jnp.stack([jnp.broadcast_to(m[:1], (B, 6 * D)), m[1:]], axis=1).reshape(B, 2, 6, D)

        qa, ka, va, rest = _inproj(xa, norm_pre[i, 0], mods, _layout_w_in(w_in[i]), cos, sin, n_ctx)

        lam_init = 0.8 - 0.6 * math.exp(-0.3 * i)
        lv = da_lambda[i].astype(F32)
        lam = jnp.exp(jnp.sum(lv[0] * lv[1])) - jnp.exp(jnp.sum(lv[2] * lv[3])) + lam_init
        a_out = _attention(qa, ka, va, lam, da_norm[i], n_ctx=n_ctx, out_scale=1.0 - lam_init)

        g_f, g_b = _scan(_gdn_prep(rest, gdn_conv[i], gdn_a_log[i], gdn_dt_bias[i], n_ctx),
                         _gdn_prep_chunk, _gdn_update, n_ctx)
        rw = _rwkv_prep(rest, rwkv_w_up[i], rwkv_a_up[i], rwkv_g_up[i], rwkv_w0[i], rwkv_a0[i],
                        rwkv_k_k[i], rwkv_k_a[i], rwkv_r_k[i])
        r_f, r_b = _scan(rw, _rwkv_prep_chunk, _rwkv_update, n_ctx)

        xa = _outproj(a_out, g_f, g_b, r_f, r_b, rest, rw, xa, w_out[i].astype(BF16), gdn_norm[i],
                      rwkv_gn_w[i], rwkv_gn_b[i], norm_post[i, 0], mods, n_ctx)

        h, aff = _router(xa, norm_pre[i, 1], mods, moe_router[i], n_ctx)
        moe = functools.partial(_moe, wg=moe_w_gate, wu=moe_w_up, wd=moe_w_down, norm_g=norm_post[i, 1],
                                mods=mods, layer=i)
        xa = moe(xa, h, aff, row0=n_ctx, n_tok=T, mod_row=1, only_routed=last)
        if not last:
            xa = moe(xa, h, aff, row0=0, n_tok=n_ctx, mod_row=0)
    return xa
```
